```python
import jax, jax.numpy as jnp
from jax import lax
import numpy as np

D_MODEL = 1024
BATCH = 2
SEQ = 8192
DEPTH = 4
DEC_BATCH = 32
DEC_SEQ = 4
PAST_LEN = 8192
PAGE_SIZE = 128

N_A_LAYERS = DEPTH // 2
N_B_LAYERS = DEPTH - N_A_LAYERS
RET_HEADS = 8
RET_DK = D_MODEL // RET_HEADS
RET_DV = 2 * RET_DK
RET_QK_W = RET_HEADS * RET_DK
RET_V_W = RET_HEADS * RET_DV
RET_IN_WIDTH = 2 * RET_QK_W + 2 * RET_V_W
RET_CHUNK = 128
RET_GN_EPS = 1e-5
ATT_HEADS = 8
ATT_DH = D_MODEL // ATT_HEADS
PATTERNS = ((128, 1), (512, 4), (2048, 16))
N_PATTERNS = 3
W_MAX = 2048
Q_BLOCK = 128
N_EXPERT_GROUPS = 4
EXPERTS_PER_GROUP = 4
N_EXPERTS = N_EXPERT_GROUPS * EXPERTS_PER_GROUP
TOP_K_IN_GROUP = 2
D_EXPERT = 512
PLE_DIM = 256
DEEPNORM_ALPHA = (2 * DEPTH) ** 0.25
DEEPNORM_BETA = (8 * DEPTH) ** -0.25
LN_EPS = 1e-5

kernel_name = 'yoco_retention_dilated_window_hmoe_decoder'


def layer_norm(x, g, b):
    xf = x.astype(jnp.float32)
    mu = xf.mean(-1, keepdims=True)
    var = jnp.square(xf - mu).mean(-1, keepdims=True)
    y = (xf - mu) * lax.rsqrt(var + LN_EPS) * g.astype(jnp.float32) + b.astype(jnp.float32)
    return y.astype(x.dtype)


def ret_log_gamma():
    return jnp.log1p(-jnp.exp2(-5.0 - jnp.arange(RET_HEADS, dtype=jnp.float32)))


def xpos_rotate(t, pos):
    half = RET_DK // 2
    theta = 1.0 / (10000.0 ** jnp.linspace(0.0, 1.0, half, dtype=jnp.float32))
    ang = pos.astype(jnp.float32)[:, None] * theta[None, :]
    cos = jnp.cos(ang)[None, :, None, :]
    sin = jnp.sin(ang)[None, :, None, :]
    t1, t2 = t[..., :half], t[..., half:]
    return jnp.concatenate([t1 * cos - t2 * sin, t1 * sin + t2 * cos], axis=-1)


def retention_chunk(S, q, k, v):
    L = q.shape[1]
    lg = ret_log_gamma()
    i = jnp.arange(L, dtype=jnp.float32)
    diff = i[:, None] - i[None, :]
    dmask = jnp.where(diff >= 0, jnp.exp(lg[:, None, None] * jnp.maximum(diff, 0.0)), 0.0)
    sc = jnp.einsum('bihd,bjhd->bhij', q, k) * dmask[None]
    intra = jnp.einsum('bhij,bjhe->bihe', sc, v)
    cross_decay = jnp.exp(lg[None, :] * (i[:, None] + 1.0))
    cross = jnp.einsum('bihd,bhde->bihe', q, S) * cross_decay[None, :, :, None]
    k_decay = jnp.exp(lg[None, :] * (L - 1.0 - i[:, None]))
    S_new = jnp.exp(lg * L)[None, :, None, None] * S + jnp.einsum('bjhd,bjhe->bhde', k * k_decay[None, :, :, None], v)
    return S_new, intra + cross


def retention_mixer(x, w_in, w_o, pos0, state):
    B, L, _ = x.shape
    q, k, v, g = jnp.split(x @ w_in, [RET_QK_W, 2 * RET_QK_W, 2 * RET_QK_W + RET_V_W], axis=-1)
    pos = pos0 + jnp.arange(L, dtype=jnp.int32)
    q = xpos_rotate(q.astype(jnp.float32).reshape(B, L, RET_HEADS, RET_DK), pos) * RET_DK ** -0.5
    k = xpos_rotate(k.astype(jnp.float32).reshape(B, L, RET_HEADS, RET_DK), pos)
    v = v.astype(jnp.float32).reshape(B, L, RET_HEADS, RET_DV)
    if state is None:
        nc = L // RET_CHUNK
        def to_chunks(t):
            return t.reshape((B, nc, RET_CHUNK) + t.shape[2:]).swapaxes(0, 1)
        S0 = jnp.zeros((B, RET_HEADS, RET_DK, RET_DV), jnp.float32)
        S_fin, o = lax.scan(lambda S, c: retention_chunk(S, c[0], c[1], c[2]), S0,
                            (to_chunks(q), to_chunks(k), to_chunks(v)))
        o = o.swapaxes(0, 1).reshape(B, L, RET_HEADS, RET_DV)
    else:
        S_fin, o = retention_chunk(state.astype(jnp.float32), q, k, v)
    mu = o.mean(-1, keepdims=True)
    var = jnp.square(o - mu).mean(-1, keepdims=True)
    o = (o - mu) * lax.rsqrt(var + RET_GN_EPS)
    y = o.reshape(B, L, RET_V_W).astype(x.dtype) * jax.nn.silu(g)
    return y @ w_o, S_fin


def dsw_block(qb, qpos, k_ext, v_ext, offset):
    outs, lses = [], []
    for g, (w, d) in enumerate(PATTERNS):
        n = w // d
        kpos = qpos[:, None] - d * jnp.arange(n + 1, dtype=jnp.int32)[None, :]
        idx = kpos - offset
        kg = jnp.take(k_ext, idx, axis=1)
        vg = jnp.take(v_ext, idx, axis=1)
        s = jnp.einsum('bqhd,bqkhd->bhqk', qb[:, :, g], kg).astype(jnp.float32)
        s = jnp.where((kpos >= 0)[None, None], s, -jnp.inf)
        m = s.max(-1, keepdims=True)
        p = jnp.exp(s - m)
        den = p.sum(-1, keepdims=True)
        outs.append(jnp.einsum('bhqk,bqkhd->bqhd', p / den, vg.astype(jnp.float32)))
        lses.append((m + jnp.log(den))[..., 0])
    wts = jax.nn.softmax(jnp.stack(lses), axis=0)
    return jnp.einsum('gbhq,gbqhd->bqhd', wts, jnp.stack(outs))


def dsw_mixer(x, w_q, w_o, k_ext, v_ext, offset, pos0):
    B, L, _ = x.shape
    q = (x @ w_q).reshape(B, L, N_PATTERNS, ATT_HEADS, ATT_DH) * ATT_DH ** -0.5
    if L > Q_BLOCK and L % Q_BLOCK == 0:
        nb = L // Q_BLOCK
        qb = q.reshape(B, nb, Q_BLOCK, N_PATTERNS, ATT_HEADS, ATT_DH).swapaxes(0, 1)
        starts = pos0 + Q_BLOCK * jnp.arange(nb, dtype=jnp.int32)
        o = lax.map(lambda a: dsw_block(a[0], a[1] + jnp.arange(Q_BLOCK, dtype=jnp.int32), k_ext, v_ext, offset),
                    (qb, starts))
        o = o.swapaxes(0, 1).reshape(B, L, ATT_HEADS * ATT_DH)
    else:
        o = dsw_block(q, pos0 + jnp.arange(L, dtype=jnp.int32), k_ext, v_ext, offset).reshape(B, L, ATT_HEADS * ATT_DH)
    return o.astype(x.dtype) @ w_o


def hmoe(x, w_rg, b_rg, w_re, b_re, w_up, w_down):
    B, L, D = x.shape
    xt = x.reshape(B * L, D)
    lg = (xt @ w_rg + b_rg).astype(jnp.float32)
    pg = jax.nn.softmax(lg, axis=-1)
    _, gsel = lax.top_k(lg, 1)
    p_group = jnp.take_along_axis(pg, gsel, axis=1)
    le = (xt @ w_re + b_re).astype(jnp.float32).reshape(-1, N_EXPERT_GROUPS, EXPERTS_PER_GROUP)
    le_sel = jnp.take_along_axis(le, gsel[:, :, None], axis=1)[:, 0]
    ve, esel = lax.top_k(le_sel, TOP_K_IN_GROUP)
    wgt = p_group * jax.nn.softmax(ve, axis=-1)
    eid = gsel * EXPERTS_PER_GROUP + esel
    gate = jnp.sum(jax.nn.one_hot(eid, N_EXPERTS, dtype=jnp.float32) * wgt[..., None], axis=1)
    y = jnp.zeros_like(xt)
    for e in range(N_EXPERTS):
        hg, hu = jnp.split(xt @ w_up[e], 2, axis=-1)
        y = y + gate[:, e:e + 1].astype(x.dtype) * ((jax.nn.silu(hg) * hu) @ w_down[e])
    return y.reshape(B, L, D)


def post_block(x, h, p_i, g_i, b_i, w_rg, b_rg, w_re, b_re, w_up, w_down, w_pg, w_pp):
    x = layer_norm(DEEPNORM_ALPHA * x + h, g_i[0], b_i[0])
    f = hmoe(x, w_rg, b_rg, w_re, b_re, w_up, w_down)
    ple = jax.nn.sigmoid(x @ w_pg) * (p_i @ w_pp)
    return layer_norm(DEEPNORM_ALPHA * x + f + ple, g_i[1], b_i[1])


def shared_kv(x, w_kv):
    B, L, _ = x.shape
    k, v = jnp.split(x @ w_kv, 2, axis=-1)
    return k.reshape(B, L, ATT_HEADS, ATT_DH), v.reshape(B, L, ATT_HEADS, ATT_DH)


def front_pad(t, n):
    return jnp.concatenate([jnp.zeros((t.shape[0], n) + t.shape[2:], t.dtype), t], axis=1)


def setup_inputs(seed: int = 0) -> dict:
    key = jax.random.key(seed)
    ks = jax.random.split(key, 22)
    D = D_MODEL
    wb = min(W_MAX, PAST_LEN)
    def nrm(k, shape, scale):
        return jax.random.normal(k, shape, jnp.float32) * scale
    return {
        'x_prompt': nrm(ks[0], (BATCH, SEQ, D), 1.0),
        'x_sample': nrm(ks[1], (DEC_BATCH, DEC_SEQ, D), 1.0),
        'state_ret': nrm(ks[2], (N_A_LAYERS, DEC_BATCH, RET_HEADS, RET_DK, RET_DV), 0.5),
        'cache_k': nrm(ks[3], (DEC_BATCH, wb, ATT_HEADS, ATT_DH), 1.0),
        'cache_v': nrm(ks[4], (DEC_BATCH, wb, ATT_HEADS, ATT_DH), 1.0),
        'p_prompt': nrm(ks[5], (DEPTH, BATCH, SEQ, PLE_DIM), 1.0),
        'p_sample': nrm(ks[6], (DEPTH, DEC_BATCH, DEC_SEQ, PLE_DIM), 1.0),
        'w_in_a': nrm(ks[7], (N_A_LAYERS, D, RET_IN_WIDTH), D ** -0.5),
        'w_o_a': nrm(ks[8], (N_A_LAYERS, RET_V_W, D), RET_V_W ** -0.5 * DEEPNORM_BETA),
        'w_kv': nrm(ks[9], (D, 2 * ATT_HEADS * ATT_DH), D ** -0.5),
        'w_q_b': nrm(ks[10], (N_B_LAYERS, D, N_PATTERNS * ATT_HEADS * ATT_DH), D ** -0.5),
        'w_o_b': nrm(ks[11], (N_B_LAYERS, ATT_HEADS * ATT_DH, D), (ATT_HEADS * ATT_DH) ** -0.5 * DEEPNORM_BETA),
        'ln_g': 1.0 + nrm(ks[12], (DEPTH, 2, D), 0.02),
        'ln_b': nrm(ks[13], (DEPTH, 2, D), 0.02),
        'w_route_group': nrm(ks[14], (DEPTH, D, N_EXPERT_GROUPS), D ** -0.5),
        'b_route_group': nrm(ks[15], (DEPTH, N_EXPERT_GROUPS), 0.01),
        'w_route_expert': nrm(ks[16], (DEPTH, D, N_EXPERTS), D ** -0.5),
        'b_route_expert': nrm(ks[17], (DEPTH, N_EXPERTS), 0.01),
        'w_expert_up': nrm(ks[18], (DEPTH, N_EXPERTS, D, 2 * D_EXPERT), D ** -0.5),
        'w_expert_down': nrm(ks[19], (DEPTH, N_EXPERTS, D_EXPERT, D), D_EXPERT ** -0.5 * DEEPNORM_BETA),
        'w_ple_gate': nrm(ks[20], (DEPTH, D, D), D ** -0.5),
        'w_ple_proj': nrm(ks[21], (DEPTH, PLE_DIM, D), PLE_DIM ** -0.5 * DEEPNORM_BETA),
    }


def reference(x_prompt, x_sample, state_ret, cache_k, cache_v, p_prompt, p_sample, w_in_a, w_o_a, w_kv,
              w_q_b, w_o_b, ln_g, ln_b, w_route_group, b_route_group, w_route_expert, b_route_expert,
              w_expert_up, w_expert_down, w_ple_gate, w_ple_proj):
    xp, xs = x_prompt, x_sample
    ret_p, ret_s = [], []
    for i in range(DEPTH):
        if i < N_A_LAYERS:
            hp, sp = retention_mixer(xp, w_in_a[i], w_o_a[i], 0, None)
            hs, ss = retention_mixer(xs, w_in_a[i], w_o_a[i], PAST_LEN, state_ret[i])
            ret_p.append(sp)
            ret_s.append(ss)
        else:
            j = i - N_A_LAYERS
            hp = dsw_mixer(xp, w_q_b[j], w_o_b[j], kext_p, vext_p, -W_MAX, 0)
            hs = dsw_mixer(xs, w_q_b[j], w_o_b[j], kext_s, vext_s, PAST_LEN - W_MAX, PAST_LEN)
        xp = post_block(xp, hp, p_prompt[i], ln_g[i], ln_b[i], w_route_group[i], b_route_group[i],
                        w_route_expert[i], b_route_expert[i], w_expert_up[i], w_expert_down[i],
                        w_ple_gate[i], w_ple_proj[i])
        xs = post_block(xs, hs, p_sample[i], ln_g[i], ln_b[i], w_route_group[i], b_route_group[i],
                        w_route_expert[i], b_route_expert[i], w_expert_up[i], w_expert_down[i],
                        w_ple_gate[i], w_ple_proj[i])
        if i == N_A_LAYERS - 1:
            k_p, v_p = shared_kv(xp, w_kv)
            k_s, v_s = shared_kv(xs, w_kv)
            wb_p = min(W_MAX, k_p.shape[1])
            wb_s = cache_k.shape[1]
            new_k_p, new_v_p = k_p[:, k_p.shape[1] - wb_p:], v_p[:, v_p.shape[1] - wb_p:]
            win_k = jnp.concatenate([cache_k, k_s], axis=1)
            win_v = jnp.concatenate([cache_v, v_s], axis=1)
            new_k_s, new_v_s = win_k[:, win_k.shape[1] - wb_s:], win_v[:, win_v.shape[1] - wb_s:]
            kext_p, vext_p = front_pad(k_p, W_MAX), front_pad(v_p, W_MAX)
            kext_s, vext_s = front_pad(win_k, W_MAX - wb_s), front_pad(win_v, W_MAX - wb_s)
    return (xp, xs, jnp.stack(ret_p), jnp.stack(ret_s), new_k_p, new_v_p, new_k_s, new_v_s)
```

```python
import functools

import jax
import jax.numpy as jnp
from jax import lax
from jax.experimental import pallas as pl
from jax.experimental.pallas import tpu as pltpu

F32 = jnp.float32
BF16 = jnp.bfloat16

N_HEADS = 8
HEAD_DIM = 128
RET_DV = 2 * HEAD_DIM
RET_CHUNK = 128
RET_GN_EPS = 1e-5
PATTERNS = ((128, 1), (512, 4), (2048, 16))
N_KEYS = 128
W_MAX = 2048
PAST_LEN = 8192
N_GROUPS = 4
EXPERTS_PER_GROUP = 4
N_EXPERTS = N_GROUPS * EXPERTS_PER_GROUP
D_EXPERT = 512
LN_EPS = 1e-5
LANES = 128
NEG = -1e30

VMEM_LIMIT = 48 * 1024 * 1024


def _cparams(n_axes, vmem=VMEM_LIMIT):
    return pltpu.CompilerParams(dimension_semantics=("arbitrary",) * n_axes, vmem_limit_bytes=vmem)


def _tile(n, pref):
    if n <= pref:
        return n
    t = pref
    while n % t:
        t //= 2
    return t


def _layer_norm(v, g, b):
    mu = jnp.mean(v, axis=-1, keepdims=True)
    d = v - mu
    var = jnp.mean(d * d, axis=-1, keepdims=True)
    return d * lax.rsqrt(var + LN_EPS) * g + b


def _mm_kernel(x_ref, w_ref, o_ref):
    o_ref[...] = jnp.dot(x_ref[...], w_ref[...], preferred_element_type=F32).astype(o_ref.dtype)


def matmul(x, w, out_dtype, tm=512, tn=2048):
    T, K = x.shape
    N = w.shape[1]
    tm, tn = _tile(T, tm), _tile(N, tn)
    return pl.pallas_call(
        _mm_kernel,
        grid=(N // tn, T // tm),
        in_specs=[pl.BlockSpec((tm, K), lambda j, i: (i, 0)),
                  pl.BlockSpec((K, tn), lambda j, i: (0, j))],
        out_specs=pl.BlockSpec((tm, tn), lambda j, i: (i, j)),
        out_shape=jax.ShapeDtypeStruct((T, N), out_dtype),
        compiler_params=_cparams(2),
        name="matmul",
    )(x, w)


def _ret_kernel(*refs, has_init):
    if has_init:
        (q_ref, k_ref, v_ref, g_ref, cos_ref, sin_ref, dmask_ref, cd_ref, kd_ref, gl_ref, s0_ref,
         y_ref, sfin_ref, s_scr) = refs
    else:
        (q_ref, k_ref, v_ref, g_ref, cos_ref, sin_ref, dmask_ref, cd_ref, kd_ref, gl_ref,
         y_ref, sfin_ref, s_scr) = refs
    c = pl.program_id(1)

    @pl.when(c == 0)
    def _():
        if has_init:
            s_scr[...] = s0_ref[...]
        else:
            s_scr[...] = jnp.zeros_like(s_scr)

    cosf = cos_ref[...]
    sinf = sin_ref[...]
    for h in range(N_HEADS):
        ks = slice(h * HEAD_DIM, (h + 1) * HEAD_DIM)
        vs = slice(h * RET_DV, (h + 1) * RET_DV)
        qh = q_ref[:, ks].astype(F32)
        kh = k_ref[:, ks].astype(F32)
        qr = qh * cosf + pltpu.roll(qh, HEAD_DIM // 2, 1) * sinf
        kr = kh * cosf + pltpu.roll(kh, HEAD_DIM // 2, 1) * sinf
        vh = v_ref[:, vs]
        sc = lax.dot_general(qr.astype(BF16), kr.astype(BF16), (((1,), (1,)), ((), ())),
                             preferred_element_type=F32) * dmask_ref[h]
        intra = jnp.dot(sc.astype(BF16), vh, preferred_element_type=F32)
        s_old = s_scr[h]
        cross = jnp.dot((qr * cd_ref[h]).astype(BF16), s_old.astype(BF16), preferred_element_type=F32)
        kdt = (kr * kd_ref[h]).T.astype(BF16)
        s_scr[h] = gl_ref[h] * s_old + jnp.dot(kdt, vh, preferred_element_type=F32)
        o = intra + cross
        mu = jnp.mean(o, axis=-1, keepdims=True)
        d = o - mu
        var = jnp.mean(d * d, axis=-1, keepdims=True)
        on = d * lax.rsqrt(var + RET_GN_EPS)
        gh = g_ref[:, vs].astype(F32)
        y_ref[:, vs] = (on * (gh * jax.nn.sigmoid(gh))).astype(y_ref.dtype)

    @pl.when(c == pl.num_programs(1) - 1)
    def _():
        sfin_ref[...] = s_scr[...]


def _ret_tables(l_true, pos0, n_pos):
    lc = RET_CHUNK
    lg = jnp.log1p(-jnp.exp2(-5.0 - jnp.arange(N_HEADS, dtype=F32)))
    i = jnp.arange(lc, dtype=F32)
    diff = i[:, None] - i[None, :]
    scale = HEAD_DIM ** -0.5
    dmask = jnp.where(diff >= 0, jnp.exp(lg[:, None, None] * jnp.maximum(diff, 0.0)), 0.0) * scale
    cd = jnp.exp(lg[:, None] * (i[None, :] + 1.0)) * scale
    kd = jnp.where(i[None, :] < l_true, jnp.exp(lg[:, None] * (l_true - 1.0 - i[None, :])), 0.0)
    gl = jnp.exp(lg * l_true)
    cd = jnp.broadcast_to(cd[:, :, None], (N_HEADS, lc, LANES))
    kd = jnp.broadcast_to(kd[:, :, None], (N_HEADS, lc, LANES))
    gl = jnp.broadcast_to(gl[:, None, None], (N_HEADS, 1, RET_DV))
    half = HEAD_DIM // 2
    theta = 1.0 / (10000.0 ** jnp.linspace(0.0, 1.0, half, dtype=F32))
    pos = pos0 + jnp.arange(n_pos, dtype=jnp.int32)
    ang = pos.astype(F32)[:, None] * theta[None, :]
    cos, sin = jnp.cos(ang), jnp.sin(ang)
    cosf = jnp.concatenate([cos, cos], axis=-1)
    sinf = jnp.concatenate([-sin, sin], axis=-1)
    return cosf, sinf, dmask, cd, kd, gl


def retention(qkvg, n_batch, n_chunks, l_true, pos0, state=None, layer=0):
    lc = RET_CHUNK
    qk_w = N_HEADS * HEAD_DIM
    v_w = N_HEADS * RET_DV
    cosf, sinf, dmask, cd, kd, gl = _ret_tables(l_true, pos0, n_chunks * lc)
    row = lambda b, c: b * n_chunks + c
    const3 = lambda b, c: (0, 0, 0)
    in_specs = [
        pl.BlockSpec((lc, qk_w), lambda b, c: (row(b, c), 0)),
        pl.BlockSpec((lc, qk_w), lambda b, c: (row(b, c), 1)),
        pl.BlockSpec((lc, v_w), lambda b, c: (row(b, c), 1)),
        pl.BlockSpec((lc, v_w), lambda b, c: (row(b, c), 2)),
        pl.BlockSpec((lc, LANES), lambda b, c: (c, 0)),
        pl.BlockSpec((lc, LANES), lambda b, c: (c, 0)),
        pl.BlockSpec((N_HEADS, lc, lc), const3),
        pl.BlockSpec((N_HEADS, lc, LANES), const3),
        pl.BlockSpec((N_HEADS, lc, LANES), const3),
        pl.BlockSpec((N_HEADS, 1, RET_DV), const3),
    ]
    args = [qkvg, qkvg, qkvg, qkvg, cosf, sinf, dmask, cd, kd, gl]
    if state is not None:
        in_specs.append(pl.BlockSpec((None, None, N_HEADS, HEAD_DIM, RET_DV), lambda b, c: (layer, b, 0, 0, 0)))
        args.append(state)
    y, sfin = pl.pallas_call(
        functools.partial(_ret_kernel, has_init=state is not None),
        grid=(n_batch, n_chunks),
        in_specs=in_specs,
        out_specs=[pl.BlockSpec((lc, v_w), lambda b, c: (row(b, c), 0)),
                   pl.BlockSpec((None, N_HEADS, HEAD_DIM, RET_DV), lambda b, c: (b, 0, 0, 0))],
        out_shape=[jax.ShapeDtypeStruct((n_batch * n_chunks * lc, v_w), BF16),
                   jax.ShapeDtypeStruct((n_batch, N_HEADS, HEAD_DIM, RET_DV), F32)],
        scratch_shapes=[pltpu.VMEM((N_HEADS, HEAD_DIM, RET_DV), F32)],
        compiler_params=_cparams(2),
        name="retention",
    )(*args)
    return y, sfin


def _store_decimated(res, out_ref, scr, d):
    tm, width = res.shape
    if d == 1:
        out_ref[0] = res.astype(out_ref.dtype)
        return
    nblk = width // LANES
    for cb in range(nblk):
        scr[cb] = res[:, cb * LANES:(cb + 1) * LANES]
    for r in range(d):
        for cb in range(nblk):
            out_ref[r, :, cb * LANES:(cb + 1) * LANES] = (
                scr[cb, pl.ds(r, tm // d, stride=d), :].astype(out_ref.dtype))


def _qproj_kernel(x_ref, w_ref, q0_ref, q1_ref, q2_ref, scr):
    res = jnp.dot(x_ref[...], w_ref[...], preferred_element_type=F32)
    width = N_HEADS * HEAD_DIM
    for g, out_ref in enumerate((q0_ref, q1_ref, q2_ref)):
        _store_decimated(res[:, g * width:(g + 1) * width], out_ref, scr, PATTERNS[g][1])


def _dec_spec(d, tm, width):
    return pl.BlockSpec((None, d, tm // d, width), lambda b, i: (b, 0, i, 0))


def q_projection(xb, w_q, n_batch, seq):
    width = N_HEADS * HEAD_DIM
    tm = _tile(seq, 512)
    nt = seq // tm
    dils = [d for _, d in PATTERNS]
    return pl.pallas_call(
        _qproj_kernel,
        grid=(n_batch, nt),
        in_specs=[pl.BlockSpec((tm, xb.shape[1]), lambda b, i: (b * nt + i, 0)),
                  pl.BlockSpec(w_q.shape, lambda b, i: (0, 0))],
        out_specs=[_dec_spec(d, tm, width) for d in dils],
        out_shape=[jax.ShapeDtypeStruct((n_batch, d, seq // d, width), BF16) for d in dils],
        scratch_shapes=[pltpu.VMEM((width // LANES, tm, LANES), F32)],
        compiler_params=_cparams(2),
        name="q_projection",
    )(xb, w_q)


def _kvproj_kernel(x_ref, w_ref, kv_ref, k0_ref, k1_ref, k2_ref, v0_ref, v1_ref, v2_ref, scr):
    res = jnp.dot(x_ref[...], w_ref[...], preferred_element_type=F32)
    kv_ref[...] = res
    width = N_HEADS * HEAD_DIM
    for g, (k_ref, v_ref) in enumerate(((k0_ref, v0_ref), (k1_ref, v1_ref), (k2_ref, v2_ref))):
        _store_decimated(res[:, :width], k_ref, scr, PATTERNS[g][1])
        _store_decimated(res[:, width:], v_ref, scr, PATTERNS[g][1])


def kv_projection(xb, w_kv, n_batch, seq):
    width = N_HEADS * HEAD_DIM
    tm = _tile(seq, 512)
    nt = seq // tm
    dils = [d for _, d in PATTERNS]
    dec_shapes = [jax.ShapeDtypeStruct((n_batch, d, seq // d, width), BF16) for d in dils]
    outs = pl.pallas_call(
        _kvproj_kernel,
        grid=(n_batch, nt),
        in_specs=[pl.BlockSpec((tm, xb.shape[1]), lambda b, i: (b * nt + i, 0)),
                  pl.BlockSpec(w_kv.shape, lambda b, i: (0, 0))],
        out_specs=[pl.BlockSpec((tm, 2 * width), lambda b, i: (b * nt + i, 0))]
        + [_dec_spec(d, tm, width) for d in dils] * 2,
        out_shape=[jax.ShapeDtypeStruct((n_batch * seq, 2 * width), F32)] + dec_shapes * 2,
        scratch_shapes=[pltpu.VMEM((width // LANES, tm, LANES), F32)],
        compiler_params=_cparams(2),
        name="kv_projection",
    )(xb, w_kv)
    return outs[0], outs[1:4], outs[4:7]


def _dsw_kernel(q_ref, kc_ref, kp_ref, vc_ref, vp_ref, o_ref, lse_ref):
    t = pl.program_id(2)
    n = N_KEYS
    row = lax.broadcasted_iota(jnp.int32, (n, n), 0)
    col = lax.broadcasted_iota(jnp.int32, (n, n), 1)
    mask_c = col <= row
    mask_p = jnp.logical_and(col >= row, t > 0)
    scale = HEAD_DIM ** -0.5
    nt = (((1,), (1,)), ((), ()))
    lse_all = jnp.zeros((n, LANES), F32)
    for h in range(N_HEADS):
        hs = slice(h * HEAD_DIM, (h + 1) * HEAD_DIM)
        q = q_ref[:, hs]
        s_c = lax.dot_general(q, kc_ref[:, hs], nt, preferred_element_type=F32) * scale
        s_p = lax.dot_general(q, kp_ref[:, hs], nt, preferred_element_type=F32) * scale
        s_c = jnp.where(mask_c, s_c, NEG)
        s_p = jnp.where(mask_p, s_p, NEG)
        m = jnp.maximum(jnp.max(s_c, axis=1, keepdims=True), jnp.max(s_p, axis=1, keepdims=True))
        p_c = jnp.exp(s_c - m)
        p_p = jnp.exp(s_p - m)
        den = jnp.sum(p_c, axis=1, keepdims=True) + jnp.sum(p_p, axis=1, keepdims=True)
        acc = (jnp.dot(p_c.astype(BF16), vc_ref[:, hs], preferred_element_type=F32)
               + jnp.dot(p_p.astype(BF16), vp_ref[:, hs], preferred_element_type=F32))
        o_ref[:, hs] = (acc / den).astype(o_ref.dtype)
        lse_all = jnp.where(col == h, m + jnp.log(den), lse_all)
    lse_ref[...] = lse_all


def dsw_pattern(q, k, v):
    n_batch, d, ld, width = q.shape
    n = N_KEYS
    cur = lambda b, r, t: (b, r, t, 0)
    prev = lambda b, r, t: (b, r, jnp.maximum(t - 1, 0), 0)
    blk = (None, None, n, width)
    return pl.pallas_call(
        _dsw_kernel,
        grid=(n_batch, d, ld // n),
        in_specs=[pl.BlockSpec(blk, cur), pl.BlockSpec(blk, cur), pl.BlockSpec(blk, prev),
                  pl.BlockSpec(blk, cur), pl.BlockSpec(blk, prev)],
        out_specs=[pl.BlockSpec(blk, cur), pl.BlockSpec((None, None, n, LANES), cur)],
        out_shape=[jax.ShapeDtypeStruct(q.shape, BF16),
                   jax.ShapeDtypeStruct((n_batch, d, ld, LANES), F32)],
        compiler_params=_cparams(3),
        name="dsw_pattern",
    )(q, k, k, v, v)


def _load_interleaved(src_ref, scr, d, tm):
    width = src_ref.shape[-1]
    nblk = width // LANES
    if d == 1:
        return [src_ref[0, :, cb * LANES:(cb + 1) * LANES].astype(F32) for cb in range(nblk)]
    for r in range(d):
        for cb in range(nblk):
            scr[cb, pl.ds(r, tm // d, stride=d), :] = src_ref[r, :, cb * LANES:(cb + 1) * LANES].astype(F32)
    return [scr[cb] for cb in range(nblk)]


def _merge_kernel(o0_ref, o1_ref, o2_ref, l0_ref, l1_ref, l2_ref, out_ref, scr_o, scr_l):
    tm = out_ref.shape[0]
    outs, lses = [], []
    for g, (o_ref, l_ref) in enumerate(((o0_ref, l0_ref), (o1_ref, l1_ref), (o2_ref, l2_ref))):
        d = PATTERNS[g][1]
        lses.append(_load_interleaved(l_ref, scr_l.at[g], d, tm)[0])
        outs.append(_load_interleaved(o_ref, scr_o.at[g], d, tm))
    m = jnp.maximum(jnp.maximum(lses[0], lses[1]), lses[2])
    es = [jnp.exp(l - m) for l in lses]
    inv = 1.0 / (es[0] + es[1] + es[2])
    for h in range(N_HEADS):
        acc = None
        for g in range(len(PATTERNS)):
            w = (es[g] * inv)[:, h:h + 1]
            term = w * outs[g][h]
            acc = term if acc is None else acc + term
        out_ref[:, h * HEAD_DIM:(h + 1) * HEAD_DIM] = acc.astype(out_ref.dtype)


def dsw_merge(outs, lses, n_batch, seq):
    width = N_HEADS * HEAD_DIM
    tm = _tile(seq, 512)
    nt = seq // tm
    dils = [d for _, d in PATTERNS]
    return pl.pallas_call(
        _merge_kernel,
        grid=(n_batch, nt),
        in_specs=[_dec_spec(d, tm, width) for d in dils] + [_dec_spec(d, tm, LANES) for d in dils],
        out_specs=pl.BlockSpec((tm, width), lambda b, i: (b * nt + i, 0)),
        out_shape=jax.ShapeDtypeStruct((n_batch * seq, width), BF16),
        scratch_shapes=[pltpu.VMEM((len(dils), width // LANES, tm, LANES), F32),
                        pltpu.VMEM((len(dils), 1, tm, LANES), F32)],
        compiler_params=_cparams(2),
        name="dsw_merge",
    )(*outs, *lses)


def _dsw_decode_kernel(q_ref, kc_ref, vc_ref, kn_ref, vn_ref, o_ref, *, n_new, cache_len):
    scale = HEAD_DIM ** -0.5
    for i in range(n_new):
        outs, lses = [], []
        for g, (_, d) in enumerate(PATTERNS):
            q = q_ref[i, g] * scale
            n_self = (i // d) + 1
            new_rows = [i - d * j for j in range(n_self)]
            n_old = N_KEYS + 1 - n_self
            start = cache_len + i - d * N_KEYS
            if d == 1:
                k_old = kc_ref[start:start + n_old]
                v_old = vc_ref[start:start + n_old]
            else:
                k_old = kc_ref[pl.ds(start, n_old, stride=d)]
                v_old = vc_ref[pl.ds(start, n_old, stride=d)]
            k_new = jnp.stack([kn_ref[r] for r in new_rows])
            v_new = jnp.stack([vn_ref[r] for r in new_rows])
            s_old = jnp.sum(k_old * q[None], axis=-1, keepdims=True)
            s_new = jnp.sum(k_new * q[None], axis=-1, keepdims=True)
            m = jnp.maximum(jnp.max(s_old, axis=0), jnp.max(s_new, axis=0))
            p_old = jnp.exp(s_old - m[None])
            p_new = jnp.exp(s_new - m[None])
            den = jnp.sum(p_old, axis=0) + jnp.sum(p_new, axis=0)
            acc = jnp.sum(p_old * v_old, axis=0) + jnp.sum(p_new * v_new, axis=0)
            outs.append(acc / den)
            lses.append(m + jnp.log(den))
        mm = jnp.maximum(jnp.maximum(lses[0], lses[1]), lses[2])
        es = [jnp.exp(l - mm) for l in lses]
        inv = 1.0 / (es[0] + es[1] + es[2])
        o_ref[i] = (es[0] * inv) * outs[0] + (es[1] * inv) * outs[1] + (es[2] * inv) * outs[2]


def dsw_decode(q, cache_k, cache_v, k_new, v_new):
    n_batch, n_new = q.shape[:2]
    cache_len = cache_k.shape[1]
    assert n_new <= min(d for _, d in PATTERNS[1:]) and cache_len >= max(w for w, _ in PATTERNS)
    b5 = lambda b: (b, 0, 0, 0, 0)
    b4 = lambda b: (b, 0, 0, 0)
    return pl.pallas_call(
        functools.partial(_dsw_decode_kernel, n_new=n_new, cache_len=cache_len),
        grid=(n_batch,),
        in_specs=[pl.BlockSpec((None,) + q.shape[1:], b5),
                  pl.BlockSpec((None,) + cache_k.shape[1:], b4),
                  pl.BlockSpec((None,) + cache_v.shape[1:], b4),
                  pl.BlockSpec((None,) + k_new.shape[1:], b4),
                  pl.BlockSpec((None,) + v_new.shape[1:], b4)],
        out_specs=pl.BlockSpec((None,) + k_new.shape[1:], b4),
        out_shape=jax.ShapeDtypeStruct(k_new.shape, F32),
        compiler_params=_cparams(1),
        name="dsw_decode",
    )(q, cache_k, cache_v, k_new, v_new)


def _route(logit):
    lane = lax.broadcasted_iota(jnp.int32, logit.shape, 1)
    lanef = lane.astype(F32)
    big = 1e9
    is_g = jnp.logical_and(lane >= N_EXPERTS, lane < N_EXPERTS + N_GROUPS)
    lg = jnp.where(is_g, logit, NEG)
    mg = jnp.max(lg, axis=1, keepdims=True)
    gsel = jnp.min(jnp.where(lg == mg, lanef, big), axis=1, keepdims=True) - float(N_EXPERTS)
    p_group = 1.0 / jnp.sum(jnp.exp(lg - mg), axis=1, keepdims=True)
    assert EXPERTS_PER_GROUP == 4
    egrp = jnp.right_shift(lane, 2).astype(F32)
    in_grp = jnp.logical_and(lane < N_EXPERTS, egrp == gsel)
    le = jnp.where(in_grp, logit, NEG)
    v1 = jnp.max(le, axis=1, keepdims=True)
    e1 = jnp.min(jnp.where(le == v1, lanef, big), axis=1, keepdims=True)
    le2 = jnp.where(lanef == e1, NEG, le)
    v2 = jnp.max(le2, axis=1, keepdims=True)
    e2 = jnp.min(jnp.where(jnp.logical_and(le2 == v2, lanef != e1), lanef, big), axis=1, keepdims=True)
    t = jnp.exp(v2 - v1)
    w1 = 1.0 / (1.0 + t)
    w2 = t * w1
    return (jnp.where(lanef == e1, p_group * w1, 0.0) + jnp.where(lanef == e2, p_group * w2, 0.0))


def _proj_ln_route_kernel(y_ref, wo_ref, x_ref, g_ref, b_ref, wr_ref, br_ref, x1_ref, x1b_ref, gate_ref, *, alpha):
    h = jnp.dot(y_ref[...], wo_ref[...], preferred_element_type=F32)
    x1 = _layer_norm(alpha * x_ref[...] + h, g_ref[...], b_ref[...])
    x1_ref[...] = x1
    x1b_ref[...] = x1.astype(BF16)
    logit = jnp.dot(x1, wr_ref[...], preferred_element_type=F32, precision=lax.Precision.HIGHEST) + br_ref[...]
    gate_ref[...] = _route(logit)


def proj_ln_route(y, w_o, x, g, b, w_r, b_r, alpha):
    T, D = x.shape
    tm = _tile(T, 512)
    row = lambda i: (i, 0)
    const = lambda i: (0, 0)
    return pl.pallas_call(
        functools.partial(_proj_ln_route_kernel, alpha=alpha),
        grid=(T // tm,),
        in_specs=[pl.BlockSpec((tm, y.shape[1]), row), pl.BlockSpec(w_o.shape, const),
                  pl.BlockSpec((tm, D), row), pl.BlockSpec((1, D), const), pl.BlockSpec((1, D), const),
                  pl.BlockSpec(w_r.shape, const), pl.BlockSpec((1, LANES), const)],
        out_specs=[pl.BlockSpec((tm, D), row), pl.BlockSpec((tm, D), row), pl.BlockSpec((tm, LANES), row)],
        out_shape=[jax.ShapeDtypeStruct((T, D), F32), jax.ShapeDtypeStruct((T, D), BF16),
                   jax.ShapeDtypeStruct((T, LANES), F32)],
        compiler_params=_cparams(1),
        name="proj_ln_route",
    )(y, w_o, x, g, b, w_r, b_r)


def _moe_kernel(x_ref, gate_ref, wu_ref, wd_ref, f_ref):
    e = pl.program_id(1)

    @pl.when(e == 0)
    def _():
        f_ref[...] = jnp.zeros_like(f_ref)

    h = jnp.dot(x_ref[...], wu_ref[...], preferred_element_type=F32)
    hg, hu = h[:, :D_EXPERT], h[:, D_EXPERT:]
    gate = gate_ref[...]
    lane = lax.broadcasted_iota(jnp.int32, gate.shape, 1)
    ge = jnp.sum(jnp.where(lane == e, gate, 0.0), axis=1, keepdims=True)
    a = (hg * jax.nn.sigmoid(hg) * hu * ge).astype(BF16)
    f_ref[...] += jnp.dot(a, wd_ref[...], preferred_element_type=F32)


def moe(xb, gate, w_up, w_down, layer):
    T, D = xb.shape
    tm = _tile(T, 1024)
    return pl.pallas_call(
        _moe_kernel,
        grid=(T // tm, N_EXPERTS),
        in_specs=[pl.BlockSpec((tm, D), lambda i, e: (i, 0)),
                  pl.BlockSpec((tm, LANES), lambda i, e: (i, 0)),
                  pl.BlockSpec((None, None, D, 2 * D_EXPERT), lambda i, e: (layer, e, 0, 0)),
                  pl.BlockSpec((None, None, D_EXPERT, D), lambda i, e: (layer, e, 0, 0))],
        out_specs=pl.BlockSpec((tm, D), lambda i, e: (i, 0)),
        out_shape=jax.ShapeDtypeStruct((T, D), F32),
        compiler_params=_cparams(2),
        name="moe",
    )(xb, gate, w_up, w_down)


def _post_kernel(x1_ref, x1b_ref, f_ref, p_ref, wpg_ref, wpp_ref, g_ref, b_ref, x2_ref, x2b_ref, *, alpha):
    gatev = jax.nn.sigmoid(jnp.dot(x1b_ref[...], wpg_ref[...], preferred_element_type=F32))
    pp = jnp.dot(p_ref[...].astype(BF16), wpp_ref[...], preferred_element_type=F32)
    x2 = _layer_norm(alpha * x1_ref[...] + f_ref[...] + gatev * pp, g_ref[...], b_ref[...])
    x2_ref[...] = x2
    x2b_ref[...] = x2.astype(BF16)


def post(x1, x1b, f, p, layer, w_pg, w_pp, g, b, alpha):
    T, D = x1.shape
    tm = _tile(T, 512)
    row = lambda i: (i, 0)
    const = lambda i: (0, 0)
    return pl.pallas_call(
        functools.partial(_post_kernel, alpha=alpha),
        grid=(T // tm,),
        in_specs=[pl.BlockSpec((tm, D), row), pl.BlockSpec((tm, D), row), pl.BlockSpec((tm, D), row),
                  pl.BlockSpec((None, tm, p.shape[-1]), lambda i: (layer, i, 0)),
                  pl.BlockSpec(w_pg.shape, const), pl.BlockSpec(w_pp.shape, const),
                  pl.BlockSpec((1, D), const), pl.BlockSpec((1, D), const)],
        out_specs=[pl.BlockSpec((tm, D), row), pl.BlockSpec((tm, D), row)],
        out_shape=[jax.ShapeDtypeStruct((T, D), F32), jax.ShapeDtypeStruct((T, D), BF16)],
        compiler_params=_cparams(1),
        name="post",
    )(x1, x1b, f, p, w_pg, w_pp, g, b)


def kernel(x_prompt, x_sample, state_ret, cache_k, cache_v, p_prompt, p_sample, w_in_a, w_o_a, w_kv, w_q_b, w_o_b,
           ln_g, ln_b, w_route_group, b_route_group, w_route_expert, b_route_expert, w_expert_up, w_expert_down,
           w_ple_gate, w_ple_proj):
    B, L, D = x_prompt.shape
    Bs, Ls, _ = x_sample.shape
    n_a, n_b = w_in_a.shape[0], w_q_b.shape[0]
    depth = n_a + n_b
    alpha = (2 * depth) ** 0.25
    width = N_HEADS * HEAD_DIM
    cache_len = cache_k.shape[1]
    assert D == width and L % (RET_CHUNK * PATTERNS[-1][1]) == 0 and Ls <= RET_CHUNK

    w_in_b, w_o_a_b, w_kv_b = w_in_a.astype(BF16), w_o_a.astype(BF16), w_kv.astype(BF16)
    w_q_bb, w_o_b_b = w_q_b.astype(BF16), w_o_b.astype(BF16)
    w_up_b, w_down_b = w_expert_up.astype(BF16), w_expert_down.astype(BF16)
    w_pg_b, w_pp_b = w_ple_gate.astype(BF16), w_ple_proj.astype(BF16)
    pad = LANES - N_EXPERTS - N_GROUPS
    w_r = jnp.concatenate([w_route_expert, w_route_group, jnp.zeros((depth, D, pad), F32)], axis=-1)
    b_r = jnp.concatenate([b_route_expert, b_route_group, jnp.zeros((depth, pad), F32)], axis=-1)[:, None, :]

    pp = p_prompt.reshape(depth, B * L, -1)
    ps = p_sample.reshape(depth, Bs * Ls, -1)
    streams = {
        "p": [x_prompt.reshape(B * L, D), None, pp],
        "s": [x_sample.reshape(Bs * Ls, D), None, ps],
    }
    for st in streams.values():
        st[1] = st[0].astype(BF16)

    ret_p, ret_s = [], []
    kv_state = None
    for i in range(depth):
        mixed = {}
        if i < n_a:
            qkvg = matmul(streams["p"][1], w_in_b[i], BF16)
            y, s_fin = retention(qkvg, B, L // RET_CHUNK, RET_CHUNK, 0)
            mixed["p"] = y
            ret_p.append(s_fin)
            qkvg = matmul(streams["s"][1], w_in_b[i], BF16)
            qkvg = jnp.pad(qkvg.reshape(Bs, Ls, -1), ((0, 0), (0, RET_CHUNK - Ls), (0, 0)))
            y, s_fin = retention(qkvg.reshape(Bs * RET_CHUNK, -1), Bs, 1, Ls, PAST_LEN, state=state_ret, layer=i)
            mixed["s"] = y.reshape(Bs, RET_CHUNK, -1)[:, :Ls].reshape(Bs * Ls, -1)
            ret_s.append(s_fin)
            w_o = w_o_a_b[i]
        else:
            j = i - n_a
            (k_decs, v_decs), (k_s, v_s) = kv_state
            qs = q_projection(streams["p"][1], w_q_bb[j], B, L)
            outs, lses = zip(*[dsw_pattern(qs[g], k_decs[g], v_decs[g]) for g in range(len(PATTERNS))])
            mixed["p"] = dsw_merge(outs, lses, B, L)
            q_s = matmul(streams["s"][1], w_q_bb[j], F32).reshape(Bs, Ls, len(PATTERNS), N_HEADS, HEAD_DIM)
            o_s = dsw_decode(q_s, cache_k, cache_v, k_s, v_s)
            mixed["s"] = o_s.reshape(Bs * Ls, width).astype(BF16)
            w_o = w_o_b_b[j]
        for name, st in streams.items():
            x, _, p = st
            x1, x1b, gate = proj_ln_route(mixed[name], w_o, x, ln_g[i, 0:1], ln_b[i, 0:1], w_r[i], b_r[i], alpha)
            f = moe(x1b, gate, w_up_b, w_down_b, i)
            st[0], st[1] = post(x1, x1b, f, p, i, w_pg_b[i], w_pp_b[i], ln_g[i, 1:2], ln_b[i, 1:2], alpha)
        if i == n_a - 1:
            kv_p, k_decs, v_decs = kv_projection(streams["p"][1], w_kv_b, B, L)
            kv_s = matmul(streams["s"][1], w_kv_b, F32)
            k_s = kv_s[:, :width].reshape(Bs, Ls, N_HEADS, HEAD_DIM)
            v_s = kv_s[:, width:].reshape(Bs, Ls, N_HEADS, HEAD_DIM)
            kv_state = ((k_decs, v_decs), (k_s, v_s))
            wb_p = min(W_MAX, L)
            kv_tail = kv_p.reshape(B, L, 2 * width)[:, L - wb_p:]
            new_k_p = kv_tail[..., :width].reshape(B, wb_p, N_HEADS, HEAD_DIM)
            new_v_p = kv_tail[..., width:].reshape(B, wb_p, N_HEADS, HEAD_DIM)
            new_k_s = jnp.concatenate([cache_k, k_s], axis=1)[:, Ls:]
            new_v_s = jnp.concatenate([cache_v, v_s], axis=1)[:, Ls:]

    return (streams["p"][0].reshape(B, L, D), streams["s"][0].reshape(Bs, Ls, D),
            jnp.stack(ret_p), jnp.stack(ret_s), new_k_p, new_v_p, new_k_s, new_v_s)
```

```python
import functools

import jax
import jax.numpy as jnp
from jax import lax
from jax.experimental import pallas as pl
from jax.experimental.pallas import tpu as pltpu

F32 = jnp.float32
BF16 = jnp.bfloat16

N_HEADS = 8
HEAD_DIM = 128
RET_DV = 2 * HEAD_DIM
RET_CHUNK = 128
RET_GN_EPS = 1e-5
PATTERNS = ((128, 1), (512, 4), (2048, 16))
N_KEYS = 128
DSW_TQ = 256
MOE_TM = 512
W_MAX = 2048
PAST_LEN = 8192
N_GROUPS = 4
EXPERTS_PER_GROUP = 4
N_EXPERTS = N_GROUPS * EXPERTS_PER_GROUP
D_EXPERT = 512
LN_EPS = 1e-5
LANES = 128
NEG = -1e30
GROUP_LANE = N_EXPERTS

VMEM_LIMIT = 48 * 1024 * 1024


def _cparams(n_axes, vmem=VMEM_LIMIT):
    return pltpu.CompilerParams(dimension_semantics=("arbitrary",) * n_axes, vmem_limit_bytes=vmem)


def _tile(n, pref):
    if n <= pref:
        return n
    t = pref
    while n % t:
        t //= 2
    return t


def _layer_norm(v, g, b):
    mu = jnp.mean(v, axis=-1, keepdims=True)
    d = v - mu
    var = jnp.mean(d * d, axis=-1, keepdims=True)
    return d * lax.rsqrt(var + LN_EPS) * g + b


def _mm_kernel(x_ref, w_ref, o_ref):
    o_ref[...] = jnp.dot(x_ref[...], w_ref[...], preferred_element_type=F32).astype(o_ref.dtype)


def matmul(x, w, out_dtype, tm=512, tn=2048):
    T, K = x.shape
    N = w.shape[1]
    tm, tn = _tile(T, tm), _tile(N, tn)
    return pl.pallas_call(
        _mm_kernel,
        grid=(N // tn, T // tm),
        in_specs=[pl.BlockSpec((tm, K), lambda j, i: (i, 0)),
                  pl.BlockSpec((K, tn), lambda j, i: (0, j))],
        out_specs=pl.BlockSpec((tm, tn), lambda j, i: (i, j)),
        out_shape=jax.ShapeDtypeStruct((T, N), out_dtype),
        compiler_params=_cparams(2),
        name="matmul",
    )(x, w)


def _ret_kernel(*refs, has_init):
    if has_init:
        (q_ref, k_ref, v_ref, g_ref, cos_ref, sin_ref, dmask_ref, cd_ref, kd_ref, gl_ref, s0_ref,
         y_ref, sfin_ref, s_scr) = refs
    else:
        (q_ref, k_ref, v_ref, g_ref, cos_ref, sin_ref, dmask_ref, cd_ref, kd_ref, gl_ref,
         y_ref, sfin_ref, s_scr) = refs
    c = pl.program_id(1)

    @pl.when(c == 0)
    def _():
        if has_init:
            s_scr[...] = s0_ref[...]
        else:
            s_scr[...] = jnp.zeros_like(s_scr)

    cosf = cos_ref[...]
    sinf = sin_ref[...]
    for h in range(N_HEADS):
        ks = slice(h * HEAD_DIM, (h + 1) * HEAD_DIM)
        vs = slice(h * RET_DV, (h + 1) * RET_DV)
        qh = q_ref[:, ks].astype(F32)
        kh = k_ref[:, ks].astype(F32)
        qr = qh * cosf + pltpu.roll(qh, HEAD_DIM // 2, 1) * sinf
        kr = kh * cosf + pltpu.roll(kh, HEAD_DIM // 2, 1) * sinf
        vh = v_ref[:, vs]
        sc = lax.dot_general(qr.astype(BF16), kr.astype(BF16), (((1,), (1,)), ((), ())),
                             preferred_element_type=F32) * dmask_ref[h]
        intra = jnp.dot(sc.astype(BF16), vh, preferred_element_type=F32)
        s_old = s_scr[h]
        cross = jnp.dot((qr * cd_ref[h]).astype(BF16), s_old.astype(BF16), preferred_element_type=F32)
        kdt = (kr * kd_ref[h]).T.astype(BF16)
        s_scr[h] = gl_ref[h] * s_old + jnp.dot(kdt, vh, preferred_element_type=F32)
        o = intra + cross
        mu = jnp.mean(o, axis=-1, keepdims=True)
        d = o - mu
        var = jnp.mean(d * d, axis=-1, keepdims=True)
        on = d * lax.rsqrt(var + RET_GN_EPS)
        gh = g_ref[:, vs].astype(F32)
        y_ref[:, vs] = (on * (gh * jax.nn.sigmoid(gh))).astype(y_ref.dtype)

    @pl.when(c == pl.num_programs(1) - 1)
    def _():
        sfin_ref[...] = s_scr[...]


def _ret_tables(l_true, pos0, n_pos):
    lc = RET_CHUNK
    lg = jnp.log1p(-jnp.exp2(-5.0 - jnp.arange(N_HEADS, dtype=F32)))
    i = jnp.arange(lc, dtype=F32)
    diff = i[:, None] - i[None, :]
    scale = HEAD_DIM ** -0.5
    dmask = jnp.where(diff >= 0, jnp.exp(lg[:, None, None] * jnp.maximum(diff, 0.0)), 0.0) * scale
    cd = jnp.exp(lg[:, None] * (i[None, :] + 1.0)) * scale
    kd = jnp.where(i[None, :] < l_true, jnp.exp(lg[:, None] * (l_true - 1.0 - i[None, :])), 0.0)
    gl = jnp.exp(lg * l_true)
    cd = jnp.broadcast_to(cd[:, :, None], (N_HEADS, lc, LANES))
    kd = jnp.broadcast_to(kd[:, :, None], (N_HEADS, lc, LANES))
    gl = jnp.broadcast_to(gl[:, None, None], (N_HEADS, 1, RET_DV))
    half = HEAD_DIM // 2
    theta = 1.0 / (10000.0 ** jnp.linspace(0.0, 1.0, half, dtype=F32))
    pos = pos0 + jnp.arange(n_pos, dtype=jnp.int32)
    ang = pos.astype(F32)[:, None] * theta[None, :]
    cos, sin = jnp.cos(ang), jnp.sin(ang)
    cosf = jnp.concatenate([cos, cos], axis=-1)
    sinf = jnp.concatenate([-sin, sin], axis=-1)
    return cosf, sinf, dmask, cd, kd, gl


def retention(qkvg, n_batch, n_chunks, l_true, pos0, state=None, layer=0):
    lc = RET_CHUNK
    qk_w = N_HEADS * HEAD_DIM
    v_w = N_HEADS * RET_DV
    cosf, sinf, dmask, cd, kd, gl = _ret_tables(l_true, pos0, n_chunks * lc)
    row = lambda b, c: b * n_chunks + c
    const3 = lambda b, c: (0, 0, 0)
    in_specs = [
        pl.BlockSpec((lc, qk_w), lambda b, c: (row(b, c), 0)),
        pl.BlockSpec((lc, qk_w), lambda b, c: (row(b, c), 1)),
        pl.BlockSpec((lc, v_w), lambda b, c: (row(b, c), 1)),
        pl.BlockSpec((lc, v_w), lambda b, c: (row(b, c), 2)),
        pl.BlockSpec((lc, LANES), lambda b, c: (c, 0)),
        pl.BlockSpec((lc, LANES), lambda b, c: (c, 0)),
        pl.BlockSpec((N_HEADS, lc, lc), const3),
        pl.BlockSpec((N_HEADS, lc, LANES), const3),
        pl.BlockSpec((N_HEADS, lc, LANES), const3),
        pl.BlockSpec((N_HEADS, 1, RET_DV), const3),
    ]
    args = [qkvg, qkvg, qkvg, qkvg, cosf, sinf, dmask, cd, kd, gl]
    if state is not None:
        in_specs.append(pl.BlockSpec((None, None, N_HEADS, HEAD_DIM, RET_DV), lambda b, c: (layer, b, 0, 0, 0)))
        args.append(state)
    y, sfin = pl.pallas_call(
        functools.partial(_ret_kernel, has_init=state is not None),
        grid=(n_batch, n_chunks),
        in_specs=in_specs,
        out_specs=[pl.BlockSpec((lc, v_w), lambda b, c: (row(b, c), 0)),
                   pl.BlockSpec((None, N_HEADS, HEAD_DIM, RET_DV), lambda b, c: (b, 0, 0, 0))],
        out_shape=[jax.ShapeDtypeStruct((n_batch * n_chunks * lc, v_w), BF16),
                   jax.ShapeDtypeStruct((n_batch, N_HEADS, HEAD_DIM, RET_DV), F32)],
        scratch_shapes=[pltpu.VMEM((N_HEADS, HEAD_DIM, RET_DV), F32)],
        compiler_params=_cparams(2),
        name="retention",
    )(*args)
    return y, sfin


def _store_decimated(res, out_ref, scr, d):
    tm, width = res.shape
    if d == 1:
        out_ref[0] = res.astype(out_ref.dtype)
        return
    nblk = width // LANES
    for cb in range(nblk):
        scr[cb] = res[:, cb * LANES:(cb + 1) * LANES]
    for r in range(d):
        for cb in range(nblk):
            out_ref[r, :, cb * LANES:(cb + 1) * LANES] = (
                scr[cb, pl.ds(r, tm // d, stride=d), :].astype(out_ref.dtype))


def _qproj_kernel(x_ref, w_ref, q0_ref, q1_ref, q2_ref, scr):
    res = jnp.dot(x_ref[...], w_ref[...], preferred_element_type=F32)
    width = N_HEADS * HEAD_DIM
    for g, out_ref in enumerate((q0_ref, q1_ref, q2_ref)):
        _store_decimated(res[:, g * width:(g + 1) * width], out_ref, scr, PATTERNS[g][1])


def _dec_spec(d, tm, width):
    return pl.BlockSpec((None, d, tm // d, width), lambda b, i: (b, 0, i, 0))


def q_projection(xb, w_q, n_batch, seq):
    width = N_HEADS * HEAD_DIM
    tm = _tile(seq, 512)
    nt = seq // tm
    dils = [d for _, d in PATTERNS]
    return pl.pallas_call(
        _qproj_kernel,
        grid=(n_batch, nt),
        in_specs=[pl.BlockSpec((tm, xb.shape[1]), lambda b, i: (b * nt + i, 0)),
                  pl.BlockSpec(w_q.shape, lambda b, i: (0, 0))],
        out_specs=[_dec_spec(d, tm, width) for d in dils],
        out_shape=[jax.ShapeDtypeStruct((n_batch, d, seq // d, width), BF16) for d in dils],
        scratch_shapes=[pltpu.VMEM((width // LANES, tm, LANES), F32)],
        compiler_params=_cparams(2),
        name="q_projection",
    )(xb, w_q)


def _kvproj_kernel(x_ref, w_ref, kv_ref, k0_ref, k1_ref, k2_ref, v0_ref, v1_ref, v2_ref, scr):
    res = jnp.dot(x_ref[...], w_ref[...], preferred_element_type=F32)
    kv_ref[...] = res
    width = N_HEADS * HEAD_DIM
    for g, (k_ref, v_ref) in enumerate(((k0_ref, v0_ref), (k1_ref, v1_ref), (k2_ref, v2_ref))):
        _store_decimated(res[:, :width], k_ref, scr, PATTERNS[g][1])
        _store_decimated(res[:, width:], v_ref, scr, PATTERNS[g][1])


def kv_projection(xb, w_kv, n_batch, seq):
    width = N_HEADS * HEAD_DIM
    tm = _tile(seq, 512)
    nt = seq // tm
    dils = [d for _, d in PATTERNS]
    dec_shapes = [jax.ShapeDtypeStruct((n_batch, d, seq // d, width), BF16) for d in dils]
    outs = pl.pallas_call(
        _kvproj_kernel,
        grid=(n_batch, nt),
        in_specs=[pl.BlockSpec((tm, xb.shape[1]), lambda b, i: (b * nt + i, 0)),
                  pl.BlockSpec(w_kv.shape, lambda b, i: (0, 0))],
        out_specs=[pl.BlockSpec((tm, 2 * width), lambda b, i: (b * nt + i, 0))]
        + [_dec_spec(d, tm, width) for d in dils] * 2,
        out_shape=[jax.ShapeDtypeStruct((n_batch * seq, 2 * width), F32)] + dec_shapes * 2,
        scratch_shapes=[pltpu.VMEM((width // LANES, tm, LANES), F32)],
        compiler_params=_cparams(2),
        name="kv_projection",
    )(xb, w_kv)
    return outs[0], outs[1:4], outs[4:7]


def _dsw_kernel(q_ref, kc_ref, kp_ref, vc_ref, vp_ref, o_ref, lse_ref, k_win, v_win):
    t = pl.program_id(2)
    n = N_KEYS
    k_win[0:n] = kp_ref[...]
    k_win[n:] = kc_ref[...]
    v_win[0:n] = vp_ref[...]
    v_win[n:] = vc_ref[...]
    row = lax.broadcasted_iota(jnp.int32, (n, 2 * n), 0)
    col = lax.broadcasted_iota(jnp.int32, (n, 2 * n), 1)
    back = row + n - col
    band = jnp.logical_and(back >= 0, back <= n)
    first = jnp.logical_and(band, jnp.logical_or(col >= n, t > 0))
    lane = lax.broadcasted_iota(jnp.int32, (n, LANES), 1)
    scale = HEAD_DIM ** -0.5
    nt = (((1,), (1,)), ((), ()))
    heads = [slice(h * HEAD_DIM, (h + 1) * HEAD_DIM) for h in range(N_HEADS)]
    for sb in range(DSW_TQ // n):
        qs = slice(sb * n, (sb + 1) * n)
        ws = slice(sb * n, sb * n + 2 * n)
        mask = first if sb == 0 else band
        scores = [lax.dot_general(q_ref[qs, hs], k_win[ws, hs], nt, preferred_element_type=F32) for hs in heads]
        probs, dens, lse_all = [], [], jnp.zeros((n, LANES), F32)
        for h, s in enumerate(scores):
            s = jnp.where(mask, s * scale, NEG)
            m = jnp.max(s, axis=1, keepdims=True)
            p = jnp.exp(s - m)
            den = jnp.sum(p, axis=1, keepdims=True)
            probs.append(p.astype(BF16))
            dens.append(den)
            lse_all = jnp.where(lane == h, m + jnp.log(den), lse_all)
        for hs, p, den in zip(heads, probs, dens):
            acc = jnp.dot(p, v_win[ws, hs], preferred_element_type=F32)
            o_ref[qs, hs] = (acc / den).astype(o_ref.dtype)
        lse_ref[qs, :] = lse_all


def dsw_pattern(q, k, v):
    n_batch, d, ld, width = q.shape
    n = N_KEYS
    ratio = DSW_TQ // n
    cur = lambda b, r, t: (b, r, t, 0)
    prev = lambda b, r, t: (b, r, jnp.maximum(t * ratio - 1, 0), 0)
    blk = (None, None, DSW_TQ, width)
    blk_prev = (None, None, n, width)
    return pl.pallas_call(
        _dsw_kernel,
        grid=(n_batch, d, ld // DSW_TQ),
        in_specs=[pl.BlockSpec(blk, cur), pl.BlockSpec(blk, cur), pl.BlockSpec(blk_prev, prev),
                  pl.BlockSpec(blk, cur), pl.BlockSpec(blk_prev, prev)],
        out_specs=[pl.BlockSpec(blk, cur), pl.BlockSpec((None, None, DSW_TQ, LANES), cur)],
        out_shape=[jax.ShapeDtypeStruct(q.shape, BF16),
                   jax.ShapeDtypeStruct((n_batch, d, ld, LANES), F32)],
        scratch_shapes=[pltpu.VMEM((DSW_TQ + n, width), BF16), pltpu.VMEM((DSW_TQ + n, width), BF16)],
        compiler_params=_cparams(3),
        name="dsw_pattern",
    )(q, k, k, v, v)


def _load_interleaved(src_ref, scr, d, tm):
    width = src_ref.shape[-1]
    nblk = width // LANES
    if d == 1:
        return [src_ref[0, :, cb * LANES:(cb + 1) * LANES].astype(F32) for cb in range(nblk)]
    for r in range(d):
        for cb in range(nblk):
            scr[cb, pl.ds(r, tm // d, stride=d), :] = src_ref[r, :, cb * LANES:(cb + 1) * LANES].astype(F32)
    return [scr[cb] for cb in range(nblk)]


def _merge_kernel(o0_ref, o1_ref, o2_ref, l0_ref, l1_ref, l2_ref, out_ref, scr_o, scr_l):
    tm = out_ref.shape[0]
    outs, lses = [], []
    for g, (o_ref, l_ref) in enumerate(((o0_ref, l0_ref), (o1_ref, l1_ref), (o2_ref, l2_ref))):
        d = PATTERNS[g][1]
        lses.append(_load_interleaved(l_ref, scr_l.at[g], d, tm)[0])
        outs.append(_load_interleaved(o_ref, scr_o.at[g], d, tm))
    m = jnp.maximum(jnp.maximum(lses[0], lses[1]), lses[2])
    es = [jnp.exp(l - m) for l in lses]
    inv = 1.0 / (es[0] + es[1] + es[2])
    for h in range(N_HEADS):
        acc = None
        for g in range(len(PATTERNS)):
            w = (es[g] * inv)[:, h:h + 1]
            term = w * outs[g][h]
            acc = term if acc is None else acc + term
        out_ref[:, h * HEAD_DIM:(h + 1) * HEAD_DIM] = acc.astype(out_ref.dtype)


def dsw_merge(outs, lses, n_batch, seq):
    width = N_HEADS * HEAD_DIM
    tm = _tile(seq, 512)
    nt = seq // tm
    dils = [d for _, d in PATTERNS]
    return pl.pallas_call(
        _merge_kernel,
        grid=(n_batch, nt),
        in_specs=[_dec_spec(d, tm, width) for d in dils] + [_dec_spec(d, tm, LANES) for d in dils],
        out_specs=pl.BlockSpec((tm, width), lambda b, i: (b * nt + i, 0)),
        out_shape=jax.ShapeDtypeStruct((n_batch * seq, width), BF16),
        scratch_shapes=[pltpu.VMEM((len(dils), width // LANES, tm, LANES), F32),
                        pltpu.VMEM((len(dils), 1, tm, LANES), F32)],
        compiler_params=_cparams(2),
        name="dsw_merge",
    )(*outs, *lses)


def _dsw_decode_kernel(q_ref, kc_ref, vc_ref, kn_ref, vn_ref, o_ref, *, n_new, cache_len):
    scale = HEAD_DIM ** -0.5
    for i in range(n_new):
        outs, lses = [], []
        for g, (_, d) in enumerate(PATTERNS):
            q = q_ref[i, g] * scale
            n_self = (i // d) + 1
            new_rows = [i - d * j for j in range(n_self)]
            n_old = N_KEYS + 1 - n_self
            start = cache_len + i - d * N_KEYS
            if d == 1:
                k_old = kc_ref[start:start + n_old]
                v_old = vc_ref[start:start + n_old]
            else:
                k_old = kc_ref[pl.ds(start, n_old, stride=d)]
                v_old = vc_ref[pl.ds(start, n_old, stride=d)]
            k_new = jnp.stack([kn_ref[r] for r in new_rows])
            v_new = jnp.stack([vn_ref[r] for r in new_rows])
            s_old = jnp.sum(k_old * q[None], axis=-1, keepdims=True)
            s_new = jnp.sum(k_new * q[None], axis=-1, keepdims=True)
            m = jnp.maximum(jnp.max(s_old, axis=0), jnp.max(s_new, axis=0))
            p_old = jnp.exp(s_old - m[None])
            p_new = jnp.exp(s_new - m[None])
            den = jnp.sum(p_old, axis=0) + jnp.sum(p_new, axis=0)
            acc = jnp.sum(p_old * v_old, axis=0) + jnp.sum(p_new * v_new, axis=0)
            outs.append(acc / den)
            lses.append(m + jnp.log(den))
        mm = jnp.maximum(jnp.maximum(lses[0], lses[1]), lses[2])
        es = [jnp.exp(l - mm) for l in lses]
        inv = 1.0 / (es[0] + es[1] + es[2])
        o_ref[i] = (es[0] * inv) * outs[0] + (es[1] * inv) * outs[1] + (es[2] * inv) * outs[2]


def dsw_decode(q, cache_k, cache_v, k_new, v_new):
    n_batch, n_new = q.shape[:2]
    cache_len = cache_k.shape[1]
    assert n_new <= min(d for _, d in PATTERNS[1:]) and cache_len >= max(w for w, _ in PATTERNS)
    b5 = lambda b: (b, 0, 0, 0, 0)
    b4 = lambda b: (b, 0, 0, 0)
    return pl.pallas_call(
        functools.partial(_dsw_decode_kernel, n_new=n_new, cache_len=cache_len),
        grid=(n_batch,),
        in_specs=[pl.BlockSpec((None,) + q.shape[1:], b5),
                  pl.BlockSpec((None,) + cache_k.shape[1:], b4),
                  pl.BlockSpec((None,) + cache_v.shape[1:], b4),
                  pl.BlockSpec((None,) + k_new.shape[1:], b4),
                  pl.BlockSpec((None,) + v_new.shape[1:], b4)],
        out_specs=pl.BlockSpec((None,) + k_new.shape[1:], b4),
        out_shape=jax.ShapeDtypeStruct(k_new.shape, F32),
        compiler_params=_cparams(1),
        name="dsw_decode",
    )(q, cache_k, cache_v, k_new, v_new)


def _route(logit):
    lane = lax.broadcasted_iota(jnp.int32, logit.shape, 1)
    lanef = lane.astype(F32)
    big = 1e9
    is_g = jnp.logical_and(lane >= N_EXPERTS, lane < N_EXPERTS + N_GROUPS)
    lg = jnp.where(is_g, logit, NEG)
    mg = jnp.max(lg, axis=1, keepdims=True)
    gsel = jnp.min(jnp.where(lg == mg, lanef, big), axis=1, keepdims=True) - float(N_EXPERTS)
    p_group = 1.0 / jnp.sum(jnp.exp(lg - mg), axis=1, keepdims=True)
    assert EXPERTS_PER_GROUP == 4
    egrp = jnp.right_shift(lane, 2).astype(F32)
    in_grp = jnp.logical_and(lane < N_EXPERTS, egrp == gsel)
    le = jnp.where(in_grp, logit, NEG)
    v1 = jnp.max(le, axis=1, keepdims=True)
    e1 = jnp.min(jnp.where(le == v1, lanef, big), axis=1, keepdims=True)
    le2 = jnp.where(lanef == e1, NEG, le)
    v2 = jnp.max(le2, axis=1, keepdims=True)
    e2 = jnp.min(jnp.where(jnp.logical_and(le2 == v2, lanef != e1), lanef, big), axis=1, keepdims=True)
    t = jnp.exp(v2 - v1)
    w1 = 1.0 / (1.0 + t)
    w2 = t * w1
    gate = jnp.where(lanef == e1, p_group * w1, 0.0) + jnp.where(lanef == e2, p_group * w2, 0.0)
    return jnp.where(lane == GROUP_LANE, gsel, gate)


def _proj_ln_route_kernel(y_ref, wo_ref, x_ref, g_ref, b_ref, wr_ref, br_ref, xg_ref, *, alpha):
    d_model = x_ref.shape[1]
    h = jnp.dot(y_ref[...], wo_ref[...], preferred_element_type=F32)
    x1 = _layer_norm(alpha * x_ref[...] + h, g_ref[...], b_ref[...])
    xg_ref[:, :d_model] = x1
    hi = x1.astype(BF16)
    lo = (x1 - hi.astype(F32)).astype(BF16)
    part = jnp.dot(jnp.concatenate([hi, lo], axis=1), wr_ref[...], preferred_element_type=F32)
    logit = part[:, :LANES] + part[:, LANES:] + br_ref[...]
    xg_ref[:, d_model:] = _route(logit)


def proj_ln_route(y, w_o, x, g, b, w_r2, b_r, alpha):
    T, D = x.shape
    tm = _tile(T, 512)
    row = lambda i: (i, 0)
    const = lambda i: (0, 0)
    return pl.pallas_call(
        functools.partial(_proj_ln_route_kernel, alpha=alpha),
        grid=(T // tm,),
        in_specs=[pl.BlockSpec((tm, y.shape[1]), row), pl.BlockSpec(w_o.shape, const),
                  pl.BlockSpec((tm, D), row), pl.BlockSpec((1, D), const), pl.BlockSpec((1, D), const),
                  pl.BlockSpec(w_r2.shape, const), pl.BlockSpec((1, LANES), const)],
        out_specs=pl.BlockSpec((tm, D + LANES), row),
        out_shape=jax.ShapeDtypeStruct((T, D + LANES), F32),
        compiler_params=_cparams(1),
        name="proj_ln_route",
    )(y, w_o, x, g, b, w_r2, b_r)


def _expert_ffn(xb, gate, e, wu, wd):
    h = jnp.dot(xb, wu, preferred_element_type=F32)
    hg, hu = h[:, :D_EXPERT], h[:, D_EXPERT:]
    lane = lax.broadcasted_iota(jnp.int32, gate.shape, 1)
    ge = jnp.sum(jnp.where(lane == e, gate, 0.0), axis=1, keepdims=True)
    a = (hg * jax.nn.sigmoid(hg) * hu * ge).astype(BF16)
    return jnp.dot(a, wd, preferred_element_type=F32)


def _moe_dense_kernel(x_ref, gate_ref, wu_ref, wd_ref, f_ref):
    e = pl.program_id(1)

    @pl.when(e == 0)
    def _():
        f_ref[...] = jnp.zeros_like(f_ref)

    f_ref[...] += _expert_ffn(x_ref[...].astype(BF16), gate_ref[...], e, wu_ref[...], wd_ref[...])


def moe_dense(xg, w_up, w_down, layer):
    T, D = xg.shape[0], w_up.shape[2]
    tm = _tile(T, 1024)
    return pl.pallas_call(
        _moe_dense_kernel,
        grid=(T // tm, N_EXPERTS),
        in_specs=[pl.BlockSpec((tm, D), lambda i, e: (i, 0)),
                  pl.BlockSpec((tm, LANES), lambda i, e: (i, D // LANES)),
                  pl.BlockSpec((None, None, D, 2 * D_EXPERT), lambda i, e: (layer, e, 0, 0)),
                  pl.BlockSpec((None, None, D_EXPERT, D), lambda i, e: (layer, e, 0, 0))],
        out_specs=pl.BlockSpec((tm, D), lambda i, e: (i, 0)),
        out_shape=jax.ShapeDtypeStruct((T, D), F32),
        compiler_params=_cparams(2),
        name="moe_dense",
    )(xg, xg, w_up, w_down)


def _moe_plan(gsel, tm):
    T = gsel.shape[0]
    nt = T // tm + N_GROUPS
    groups = jnp.arange(N_GROUPS, dtype=jnp.int32)
    order = jnp.argsort(gsel, stable=True).astype(jnp.int32)
    counts = jnp.sum((gsel[:, None] == groups[None, :]).astype(jnp.int32), axis=0)
    cstart = jnp.cumsum(counts) - counts
    ntile = (counts + tm - 1) // tm
    tend = jnp.cumsum(ntile)
    tile = jnp.arange(nt, dtype=jnp.int32)
    tg = jnp.sum((tile[:, None] >= tend[None, :]).astype(jnp.int32), axis=1)
    used = tg < N_GROUPS
    tg = jnp.minimum(tg, N_GROUPS - 1)
    base = (tile - (tend - ntile)[tg]) * tm
    rows = jnp.where(used, jnp.clip(counts[tg] - base, 0, tm), 0).astype(jnp.int32)
    r = jnp.arange(tm, dtype=jnp.int32)
    pos = jnp.clip(cstart[tg][:, None] + base[:, None] + r[None, :], 0, T - 1)
    return tg.astype(jnp.int32), rows, order[pos].reshape(nt * tm)


def _moe_sorted_kernel(tg_ref, rows_ref, idx_ref, xg_hbm, wu_ref, wd_ref, y_hbm,
                       xbuf, xb16, acc, gsem, ssem, *, tm, n_tiles):
    j = pl.program_id(0)
    e = pl.program_id(1)
    slot = j % 2
    d_model = xb16.shape[1]

    def gather(tile, s):
        def body(r, carry):
            tok = idx_ref[tile * tm + r]
            pltpu.make_async_copy(xg_hbm.at[pl.ds(tok, 1)], xbuf.at[s, pl.ds(r, 1)], gsem.at[s]).start()
            return carry
        lax.fori_loop(0, tm, body, 0, unroll=8)

    def gather_wait(s):
        pltpu.make_async_copy(xg_hbm.at[pl.ds(0, tm)], xbuf.at[s], gsem.at[s]).wait()

    def scatter(tile, s):
        n = rows_ref[tile]

        def body(r, carry):
            tok = idx_ref[tile * tm + r]
            pltpu.make_async_copy(acc.at[s, pl.ds(r, 1)], y_hbm.at[pl.ds(tok, 1)], ssem.at[s]).start()
            return carry

        @pl.when(n == tm)
        def _():
            lax.fori_loop(0, tm, body, 0, unroll=8)

        @pl.when(n < tm)
        def _():
            lax.fori_loop(0, n, body, 0)

    def scatter_wait(tile, s):
        n = rows_ref[tile]

        @pl.when(n == tm)
        def _():
            pltpu.make_async_copy(acc.at[s], y_hbm.at[pl.ds(0, tm)], ssem.at[s]).wait()

        @pl.when(n < tm)
        def _():
            def body(r, carry):
                pltpu.make_async_copy(acc.at[s, pl.ds(0, 1)], y_hbm.at[pl.ds(0, 1)], ssem.at[s]).wait()
                return carry
            lax.fori_loop(0, n, body, 0)

    active = rows_ref[j] > 0

    @pl.when(e == 0)
    def _():
        @pl.when(jnp.logical_and(j == 0, active))
        def _():
            gather(0, 0)

        @pl.when(jnp.logical_and(j + 1 < n_tiles, rows_ref[jnp.minimum(j + 1, n_tiles - 1)] > 0))
        def _():
            gather(j + 1, 1 - slot)

        @pl.when(j >= 2)
        def _():
            scatter_wait(jnp.maximum(j - 2, 0), slot)

        @pl.when(active)
        def _():
            gather_wait(slot)
            xb16[...] = xbuf[slot, :, :d_model].astype(BF16)
            acc[slot] = jnp.zeros(acc.shape[1:], F32)

    @pl.when(active)
    def _():
        gate = xbuf[slot, :, d_model:]
        acc[slot] += _expert_ffn(xb16[...], gate, tg_ref[j] * EXPERTS_PER_GROUP + e, wu_ref[...], wd_ref[...])

    last_e = e == pl.num_programs(1) - 1

    @pl.when(jnp.logical_and(last_e, active))
    def _():
        scatter(j, slot)

    @pl.when(jnp.logical_and(last_e, j == n_tiles - 1))
    def _():
        scatter_wait(n_tiles - 2, 1 - slot)
        scatter_wait(n_tiles - 1, slot)


def moe_sorted(xg, w_up, w_down, layer):
    T, D = xg.shape[0], w_up.shape[2]
    tm = MOE_TM
    gsel = xg[:, D + GROUP_LANE].astype(jnp.int32)
    tile_group, tile_rows, idx = _moe_plan(gsel, tm)
    nt = tile_group.shape[0]
    w_idx = lambda j, e, tg, rows, idx: (layer, tg[j] * EXPERTS_PER_GROUP + e, 0, 0)
    grid_spec = pltpu.PrefetchScalarGridSpec(
        num_scalar_prefetch=3,
        grid=(nt, EXPERTS_PER_GROUP),
        in_specs=[pl.BlockSpec(memory_space=pl.ANY),
                  pl.BlockSpec((None, None, D, 2 * D_EXPERT), w_idx),
                  pl.BlockSpec((None, None, D_EXPERT, D), w_idx)],
        out_specs=pl.BlockSpec(memory_space=pl.ANY),
        scratch_shapes=[pltpu.VMEM((2, tm, D + LANES), F32), pltpu.VMEM((tm, D), BF16),
                        pltpu.VMEM((2, tm, D), F32),
                        pltpu.SemaphoreType.DMA((2,)), pltpu.SemaphoreType.DMA((2,))],
    )
    return pl.pallas_call(
        functools.partial(_moe_sorted_kernel, tm=tm, n_tiles=nt),
        grid_spec=grid_spec,
        out_shape=jax.ShapeDtypeStruct((T, D), F32),
        compiler_params=_cparams(2),
        name="moe_sorted",
    )(tile_group, tile_rows, idx, xg, w_up, w_down)


def _post_kernel(x1_ref, f_ref, p_ref, wpg_ref, wpp_ref, g_ref, b_ref, x2_ref, x2b_ref, *, alpha):
    x1 = x1_ref[...]
    gatev = jax.nn.sigmoid(jnp.dot(x1.astype(BF16), wpg_ref[...], preferred_element_type=F32))
    pp = jnp.dot(p_ref[...].astype(BF16), wpp_ref[...], preferred_element_type=F32)
    x2 = _layer_norm(alpha * x1 + f_ref[...] + gatev * pp, g_ref[...], b_ref[...])
    x2_ref[...] = x2
    x2b_ref[...] = x2.astype(BF16)


def post(xg, f, p, layer, w_pg, w_pp, g, b, alpha):
    T, D = xg.shape[0], w_pg.shape[0]
    tm = _tile(T, 512)
    row = lambda i: (i, 0)
    const = lambda i: (0, 0)
    return pl.pallas_call(
        functools.partial(_post_kernel, alpha=alpha),
        grid=(T // tm,),
        in_specs=[pl.BlockSpec((tm, D), row), pl.BlockSpec((tm, D), row),
                  pl.BlockSpec((None, tm, p.shape[-1]), lambda i: (layer, i, 0)),
                  pl.BlockSpec(w_pg.shape, const), pl.BlockSpec(w_pp.shape, const),
                  pl.BlockSpec((1, D), const), pl.BlockSpec((1, D), const)],
        out_specs=[pl.BlockSpec((tm, D), row), pl.BlockSpec((tm, D), row)],
        out_shape=[jax.ShapeDtypeStruct((T, D), F32), jax.ShapeDtypeStruct((T, D), BF16)],
        compiler_params=_cparams(1),
        name="post",
    )(xg, f, p, w_pg, w_pp, g, b)


def kernel(x_prompt, x_sample, state_ret, cache_k, cache_v, p_prompt, p_sample, w_in_a, w_o_a, w_kv, w_q_b, w_o_b,
           ln_g, ln_b, w_route_group, b_route_group, w_route_expert, b_route_expert, w_expert_up, w_expert_down,
           w_ple_gate, w_ple_proj):
    B, L, D = x_prompt.shape
    Bs, Ls, _ = x_sample.shape
    n_a, n_b = w_in_a.shape[0], w_q_b.shape[0]
    depth = n_a + n_b
    alpha = (2 * depth) ** 0.25
    width = N_HEADS * HEAD_DIM
    cache_len = cache_k.shape[1]
    assert D == width and L % (max(RET_CHUNK, DSW_TQ) * PATTERNS[-1][1]) == 0 and Ls <= RET_CHUNK

    w_in_b, w_o_a_b, w_kv_b = w_in_a.astype(BF16), w_o_a.astype(BF16), w_kv.astype(BF16)
    w_q_bb, w_o_b_b = w_q_b.astype(BF16), w_o_b.astype(BF16)
    w_up_b, w_down_b = w_expert_up.astype(BF16), w_expert_down.astype(BF16)
    w_pg_b, w_pp_b = w_ple_gate.astype(BF16), w_ple_proj.astype(BF16)
    pad = LANES - N_EXPERTS - N_GROUPS
    w_r = jnp.concatenate([w_route_expert, w_route_group, jnp.zeros((depth, D, pad), F32)], axis=-1)
    w_r_hi = w_r.astype(BF16)
    w_r_lo = (w_r - w_r_hi.astype(F32)).astype(BF16)
    w_r2 = jnp.concatenate([w_r_hi, w_r_lo], axis=-1)
    w_r2 = jnp.concatenate([w_r2, w_r2], axis=1)
    b_r =jnp.concatenate([b_route_expert, b_route_group, jnp.zeros((depth, pad), F32)], axis=-1)[:, None, :]

    pp = p_prompt.reshape(depth, B * L, -1)
    ps = p_sample.reshape(depth, Bs * Ls, -1)
    streams = {
        "p": [x_prompt.reshape(B * L, D), None, pp],
        "s": [x_sample.reshape(Bs * Ls, D), None, ps],
    }
    for st in streams.values():
        st[1] = st[0].astype(BF16)

    ret_p, ret_s = [], []
    kv_state = None
    for i in range(depth):
        mixed = {}
        if i < n_a:
            qkvg = matmul(streams["p"][1], w_in_b[i], BF16)
            y, s_fin = retention(qkvg, B, L // RET_CHUNK, RET_CHUNK, 0)
            mixed["p"] = y
            ret_p.append(s_fin)
            qkvg = matmul(streams["s"][1], w_in_b[i], BF16)
            qkvg = jnp.pad(qkvg.reshape(Bs, Ls, -1), ((0, 0), (0, RET_CHUNK - Ls), (0, 0)))
            y, s_fin = retention(qkvg.reshape(Bs * RET_CHUNK, -1), Bs, 1, Ls, PAST_LEN, state=state_ret, layer=i)
            mixed["s"] = y.reshape(Bs, RET_CHUNK, -1)[:, :Ls].reshape(Bs * Ls, -1)
            ret_s.append(s_fin)
            w_o = w_o_a_b[i]
        else:
            j = i - n_a
            (k_decs, v_decs), (k_s, v_s) = kv_state
            qs = q_projection(streams["p"][1], w_q_bb[j], B, L)
            outs, lses = zip(*[dsw_pattern(qs[g], k_decs[g], v_decs[g]) for g in range(len(PATTERNS))])
            mixed["p"] = dsw_merge(outs, lses, B, L)
            q_s = matmul(streams["s"][1], w_q_bb[j], F32).reshape(Bs, Ls, len(PATTERNS), N_HEADS, HEAD_DIM)
            o_s = dsw_decode(q_s, cache_k, cache_v, k_s, v_s)
            mixed["s"] = o_s.reshape(Bs * Ls, width).astype(BF16)
            w_o = w_o_b_b[j]
        for name, st in streams.items():
            x, _, p = st
            xg = proj_ln_route(mixed[name], w_o, x, ln_g[i, 0:1], ln_b[i, 0:1], w_r2[i], b_r[i], alpha)
            sortable = xg.shape[0] % MOE_TM == 0 and xg.shape[0] >= 8 * MOE_TM
            f = (moe_sorted if sortable else moe_dense)(xg, w_up_b, w_down_b, i)
            st[0], st[1] = post(xg, f, p, i, w_pg_b[i], w_pp_b[i], ln_g[i, 1:2], ln_b[i, 1:2], alpha)
        if i == n_a - 1:
            kv_p, k_decs, v_decs = kv_projection(streams["p"][1], w_kv_b, B, L)
            kv_s = matmul(streams["s"][1], w_kv_b, F32)
            k_s = kv_s[:, :width].reshape(Bs, Ls, N_HEADS, HEAD_DIM)
            v_s = kv_s[:, width:].reshape(Bs, Ls, N_HEADS, HEAD_DIM)
            kv_state = ((k_decs, v_decs), (k_s, v_s))
            wb_p = min(W_MAX, L)
            kv_tail = kv_p.reshape(B, L, 2 * width)[:, L - wb_p:]
            new_k_p = kv_tail[..., :width].reshape(B, wb_p, N_HEADS, HEAD_DIM)
            new_v_p = kv_tail[..., width:].reshape(B, wb_p, N_HEADS, HEAD_DIM)
            new_k_s = jnp.concatenate([cache_k, k_s], axis=1)[:, Ls:]
            new_v_s = jnp.concatenate([cache_v, v_s], axis=1)[:, Ls:]

    return (streams["p"][0].reshape(B, L, D), streams["s"][0].reshape(Bs, Ls, D),
            jnp.stack(ret_p), jnp.stack(ret_s), new_k_p, new_v_p, new_k_s, new_v_s)
```

```python
import functools

import jax
import jax.numpy as jnp
from jax import lax
from jax.experimental import pallas as pl
from jax.experimental.pallas import tpu as pltpu

F32 = jnp.float32
BF16 = jnp.bfloat16

N_HEADS = 8
HEAD_DIM = 128
RET_DV = 2 * HEAD_DIM
RET_CHUNK = 128
RET_GN_EPS = 1e-5
PATTERNS = ((128, 1), (512, 4), (2048, 16))
N_KEYS = 128
DSW_TQ = 512
MOE_TM = 512
W_MAX = 2048
PAST_LEN = 8192
N_GROUPS = 4
EXPERTS_PER_GROUP = 4
N_EXPERTS = N_GROUPS * EXPERTS_PER_GROUP
D_EXPERT = 512
LN_EPS = 1e-5
LANES = 128
NEG = -1e30
GROUP_LANE = N_EXPERTS

VMEM_LIMIT = 48 * 1024 * 1024


def _cparams(n_axes, vmem=VMEM_LIMIT):
    return pltpu.CompilerParams(dimension_semantics=("arbitrary",) * n_axes, vmem_limit_bytes=vmem)


def _tile(n, pref):
    if n <= pref:
        return n
    t = pref
    while n % t:
        t //= 2
    return t


def _layer_norm(v, g, b):
    mu = jnp.mean(v, axis=-1, keepdims=True)
    d = v - mu
    var = jnp.mean(d * d, axis=-1, keepdims=True)
    return d * lax.rsqrt(var + LN_EPS) * g + b


def _mm_kernel(x_ref, w_ref, o_ref):
    o_ref[...] = jnp.dot(x_ref[...], w_ref[...], preferred_element_type=F32).astype(o_ref.dtype)


def matmul(x, w, out_dtype, tm=512, tn=2048):
    T, K = x.shape
    N = w.shape[1]
    tm, tn = _tile(T, tm), _tile(N, tn)
    return pl.pallas_call(
        _mm_kernel,
        grid=(N // tn, T // tm),
        in_specs=[pl.BlockSpec((tm, K), lambda j, i: (i, 0)),
                  pl.BlockSpec((K, tn), lambda j, i: (0, j))],
        out_specs=pl.BlockSpec((tm, tn), lambda j, i: (i, j)),
        out_shape=jax.ShapeDtypeStruct((T, N), out_dtype),
        compiler_params=_cparams(2),
        name="matmul",
    )(x, w)


def _ret_kernel(*refs, has_init):
    if has_init:
        (q_ref, k_ref, v_ref, g_ref, cos_ref, sin_ref, dmask_ref, cd_ref, kd_ref, gl_ref, s0_ref,
         y_ref, sfin_ref, s_scr) = refs
    else:
        (q_ref, k_ref, v_ref, g_ref, cos_ref, sin_ref, dmask_ref, cd_ref, kd_ref, gl_ref,
         y_ref, sfin_ref, s_scr) = refs
    c = pl.program_id(1)

    @pl.when(c == 0)
    def _():
        if has_init:
            s_scr[...] = s0_ref[...]
        else:
            s_scr[...] = jnp.zeros_like(s_scr)

    cosf = cos_ref[...]
    sinf = sin_ref[...]
    for h in range(N_HEADS):
        ks = slice(h * HEAD_DIM, (h + 1) * HEAD_DIM)
        vs = slice(h * RET_DV, (h + 1) * RET_DV)
        qh = q_ref[:, ks].astype(F32)
        kh = k_ref[:, ks].astype(F32)
        qr = qh * cosf + pltpu.roll(qh, HEAD_DIM // 2, 1) * sinf
        kr = kh * cosf + pltpu.roll(kh, HEAD_DIM // 2, 1) * sinf
        vh = v_ref[:, vs]
        sc = lax.dot_general(qr.astype(BF16), kr.astype(BF16), (((1,), (1,)), ((), ())),
                             preferred_element_type=F32) * dmask_ref[h]
        intra = jnp.dot(sc.astype(BF16), vh, preferred_element_type=F32)
        s_old = s_scr[h]
        cross = jnp.dot((qr * cd_ref[h]).astype(BF16), s_old.astype(BF16), preferred_element_type=F32)
        kdt = (kr * kd_ref[h]).T.astype(BF16)
        s_scr[h] = gl_ref[h] * s_old + jnp.dot(kdt, vh, preferred_element_type=F32)
        o = intra + cross
        mu = jnp.mean(o, axis=-1, keepdims=True)
        d = o - mu
        var = jnp.mean(d * d, axis=-1, keepdims=True)
        on = d * lax.rsqrt(var + RET_GN_EPS)
        gh = g_ref[:, vs].astype(F32)
        y_ref[:, vs] = (on * (gh * jax.nn.sigmoid(gh))).astype(y_ref.dtype)

    @pl.when(c == pl.num_programs(1) - 1)
    def _():
        sfin_ref[...] = s_scr[...]


def _ret_tables(l_true, pos0, n_pos):
    lc = RET_CHUNK
    lg = jnp.log1p(-jnp.exp2(-5.0 - jnp.arange(N_HEADS, dtype=F32)))
    i = jnp.arange(lc, dtype=F32)
    diff = i[:, None] - i[None, :]
    scale = HEAD_DIM ** -0.5
    dmask = jnp.where(diff >= 0, jnp.exp(lg[:, None, None] * jnp.maximum(diff, 0.0)), 0.0) * scale
    cd = jnp.exp(lg[:, None] * (i[None, :] + 1.0)) * scale
    kd = jnp.where(i[None, :] < l_true, jnp.exp(lg[:, None] * (l_true - 1.0 - i[None, :])), 0.0)
    gl = jnp.exp(lg * l_true)
    cd = jnp.broadcast_to(cd[:, :, None], (N_HEADS, lc, LANES))
    kd = jnp.broadcast_to(kd[:, :, None], (N_HEADS, lc, LANES))
    gl = jnp.broadcast_to(gl[:, None, None], (N_HEADS, 1, RET_DV))
    half = HEAD_DIM // 2
    theta = 1.0 / (10000.0 ** jnp.linspace(0.0, 1.0, half, dtype=F32))
    pos = pos0 + jnp.arange(n_pos, dtype=jnp.int32)
    ang = pos.astype(F32)[:, None] * theta[None, :]
    cos, sin = jnp.cos(ang), jnp.sin(ang)
    cosf = jnp.concatenate([cos, cos], axis=-1)
    sinf = jnp.concatenate([-sin, sin], axis=-1)
    return cosf, sinf, dmask, cd, kd, gl


def retention(qkvg, n_batch, n_chunks, l_true, pos0, state=None, layer=0):
    lc = RET_CHUNK
    qk_w = N_HEADS * HEAD_DIM
    v_w = N_HEADS * RET_DV
    cosf, sinf, dmask, cd, kd, gl = _ret_tables(l_true, pos0, n_chunks * lc)
    row = lambda b, c: b * n_chunks + c
    const3 = lambda b, c: (0, 0, 0)
    in_specs = [
        pl.BlockSpec((lc, qk_w), lambda b, c: (row(b, c), 0)),
        pl.BlockSpec((lc, qk_w), lambda b, c: (row(b, c), 1)),
        pl.BlockSpec((lc, v_w), lambda b, c: (row(b, c), 1)),
        pl.BlockSpec((lc, v_w), lambda b, c: (row(b, c), 2)),
        pl.BlockSpec((lc, LANES), lambda b, c: (c, 0)),
        pl.BlockSpec((lc, LANES), lambda b, c: (c, 0)),
        pl.BlockSpec((N_HEADS, lc, lc), const3),
        pl.BlockSpec((N_HEADS, lc, LANES), const3),
        pl.BlockSpec((N_HEADS, lc, LANES), const3),
        pl.BlockSpec((N_HEADS, 1, RET_DV), const3),
    ]
    args = [qkvg, qkvg, qkvg, qkvg, cosf, sinf, dmask, cd, kd, gl]
    if state is not None:
        in_specs.append(pl.BlockSpec((None, None, N_HEADS, HEAD_DIM, RET_DV), lambda b, c: (layer, b, 0, 0, 0)))
        args.append(state)
    y, sfin = pl.pallas_call(
        functools.partial(_ret_kernel, has_init=state is not None),
        grid=(n_batch, n_chunks),
        in_specs=in_specs,
        out_specs=[pl.BlockSpec((lc, v_w), lambda b, c: (row(b, c), 0)),
                   pl.BlockSpec((None, N_HEADS, HEAD_DIM, RET_DV), lambda b, c: (b, 0, 0, 0))],
        out_shape=[jax.ShapeDtypeStruct((n_batch * n_chunks * lc, v_w), BF16),
                   jax.ShapeDtypeStruct((n_batch, N_HEADS, HEAD_DIM, RET_DV), F32)],
        scratch_shapes=[pltpu.VMEM((N_HEADS, HEAD_DIM, RET_DV), F32)],
        compiler_params=_cparams(2),
        name="retention",
    )(*args)
    return y, sfin


def _store_decimated(res, out_ref, scr, d):
    tm, width = res.shape
    if d == 1:
        out_ref[0] = res.astype(out_ref.dtype)
        return
    nblk = width // LANES
    for cb in range(nblk):
        scr[cb] = res[:, cb * LANES:(cb + 1) * LANES]
    for r in range(d):
        for cb in range(nblk):
            out_ref[r, :, cb * LANES:(cb + 1) * LANES] = (
                scr[cb, pl.ds(r, tm // d, stride=d), :].astype(out_ref.dtype))


def _qproj_kernel(x_ref, w_ref, q0_ref, q1_ref, q2_ref, scr):
    res = jnp.dot(x_ref[...], w_ref[...], preferred_element_type=F32)
    width = N_HEADS * HEAD_DIM
    for g, out_ref in enumerate((q0_ref, q1_ref, q2_ref)):
        _store_decimated(res[:, g * width:(g + 1) * width], out_ref, scr, PATTERNS[g][1])


def _dec_spec(d, tm, width):
    return pl.BlockSpec((None, d, tm // d, width), lambda b, i: (b, 0, i, 0))


def q_projection(xb, w_q, n_batch, seq):
    width = N_HEADS * HEAD_DIM
    tm = _tile(seq, 512)
    nt = seq // tm
    dils = [d for _, d in PATTERNS]
    return pl.pallas_call(
        _qproj_kernel,
        grid=(n_batch, nt),
        in_specs=[pl.BlockSpec((tm, xb.shape[1]), lambda b, i: (b * nt + i, 0)),
                  pl.BlockSpec(w_q.shape, lambda b, i: (0, 0))],
        out_specs=[_dec_spec(d, tm, width) for d in dils],
        out_shape=[jax.ShapeDtypeStruct((n_batch, d, seq // d, width), BF16) for d in dils],
        scratch_shapes=[pltpu.VMEM((width // LANES, tm, LANES), F32)],
        compiler_params=_cparams(2),
        name="q_projection",
    )(xb, w_q)


def _kvproj_kernel(x_ref, w_ref, kv_ref, k0_ref, k1_ref, k2_ref, v0_ref, v1_ref, v2_ref, scr):
    res = jnp.dot(x_ref[...], w_ref[...], preferred_element_type=F32)
    kv_ref[...] = res
    width = N_HEADS * HEAD_DIM
    for g, (k_ref, v_ref) in enumerate(((k0_ref, v0_ref), (k1_ref, v1_ref), (k2_ref, v2_ref))):
        _store_decimated(res[:, :width], k_ref, scr, PATTERNS[g][1])
        _store_decimated(res[:, width:], v_ref, scr, PATTERNS[g][1])


def kv_projection(xb, w_kv, n_batch, seq):
    width = N_HEADS * HEAD_DIM
    tm = _tile(seq, 512)
    nt = seq // tm
    dils = [d for _, d in PATTERNS]
    dec_shapes = [jax.ShapeDtypeStruct((n_batch, d, seq // d, width), BF16) for d in dils]
    outs = pl.pallas_call(
        _kvproj_kernel,
        grid=(n_batch, nt),
        in_specs=[pl.BlockSpec((tm, xb.shape[1]), lambda b, i: (b * nt + i, 0)),
                  pl.BlockSpec(w_kv.shape, lambda b, i: (0, 0))],
        out_specs=[pl.BlockSpec((tm, 2 * width), lambda b, i: (b * nt + i, 0))]
        + [_dec_spec(d, tm, width) for d in dils] * 2,
        out_shape=[jax.ShapeDtypeStruct((n_batch * seq, 2 * width), F32)] + dec_shapes * 2,
        scratch_shapes=[pltpu.VMEM((width // LANES, tm, LANES), F32)],
        compiler_params=_cparams(2),
        name="kv_projection",
    )(xb, w_kv)
    return outs[0], outs[1:4], outs[4:7]


def _dsw_kernel(q_ref, kc_ref, kp_ref, vc_ref, vp_ref, o_ref, lse_ref, k_win, v_win):
    t = pl.program_id(2)
    n = N_KEYS
    k_win[0:n] = kp_ref[...]
    k_win[n:] = kc_ref[...]
    v_win[0:n] = vp_ref[...]
    v_win[n:] = vc_ref[...]
    row = lax.broadcasted_iota(jnp.int32, (n, 2 * n), 0)
    col = lax.broadcasted_iota(jnp.int32, (n, 2 * n), 1)
    back = row + n - col
    band = jnp.logical_and(back >= 0, back <= n)
    first = jnp.logical_and(band, jnp.logical_or(col >= n, t > 0))
    lane = lax.broadcasted_iota(jnp.int32, (n, LANES), 1)
    scale = HEAD_DIM ** -0.5
    nt = (((1,), (1,)), ((), ()))
    heads = [slice(h * HEAD_DIM, (h + 1) * HEAD_DIM) for h in range(N_HEADS)]
    for sb in range(DSW_TQ // n):
        qs = slice(sb * n, (sb + 1) * n)
        ws = slice(sb * n, sb * n + 2 * n)
        mask = first if sb == 0 else band
        scores = [lax.dot_general(q_ref[qs, hs], k_win[ws, hs], nt, preferred_element_type=F32) for hs in heads]
        probs, dens, lse_all = [], [], jnp.zeros((n, LANES), F32)
        for h, s in enumerate(scores):
            s = jnp.where(mask, s * scale, NEG)
            m = jnp.max(s, axis=1, keepdims=True)
            p = jnp.exp(s - m)
            den = jnp.sum(p, axis=1, keepdims=True)
            probs.append(p.astype(BF16))
            dens.append(den)
            lse_all = jnp.where(lane == h, m + jnp.log(den), lse_all)
        for hs, p, den in zip(heads, probs, dens):
            acc = jnp.dot(p, v_win[ws, hs], preferred_element_type=F32)
            o_ref[qs, hs] = (acc / den).astype(o_ref.dtype)
        lse_ref[qs, :] = lse_all


def dsw_pattern(q, k, v):
    n_batch, d, ld, width = q.shape
    n = N_KEYS
    ratio = DSW_TQ // n
    cur = lambda b, r, t: (b, r, t, 0)
    prev = lambda b, r, t: (b, r, jnp.maximum(t * ratio - 1, 0), 0)
    blk = (None, None, DSW_TQ, width)
    blk_prev = (None, None, n, width)
    return pl.pallas_call(
        _dsw_kernel,
        grid=(n_batch, d, ld // DSW_TQ),
        in_specs=[pl.BlockSpec(blk, cur), pl.BlockSpec(blk, cur), pl.BlockSpec(blk_prev, prev),
                  pl.BlockSpec(blk, cur), pl.BlockSpec(blk_prev, prev)],
        out_specs=[pl.BlockSpec(blk, cur), pl.BlockSpec((None, None, DSW_TQ, LANES), cur)],
        out_shape=[jax.ShapeDtypeStruct(q.shape, BF16),
                   jax.ShapeDtypeStruct((n_batch, d, ld, LANES), F32)],
        scratch_shapes=[pltpu.VMEM((DSW_TQ + n, width), BF16), pltpu.VMEM((DSW_TQ + n, width), BF16)],
        compiler_params=_cparams(3),
        name="dsw_pattern",
    )(q, k, k, v, v)


def _load_interleaved(src_ref, scr, d, tm):
    width = src_ref.shape[-1]
    nblk = width // LANES
    if d == 1:
        return [src_ref[0, :, cb * LANES:(cb + 1) * LANES].astype(F32) for cb in range(nblk)]
    for r in range(d):
        for cb in range(nblk):
            scr[cb, pl.ds(r, tm // d, stride=d), :] = src_ref[r, :, cb * LANES:(cb + 1) * LANES].astype(F32)
    return [scr[cb] for cb in range(nblk)]


def _merge_patterns(o_refs, l_refs, scr_o, scr_l, tm):
    outs, lses = [], []
    for g, (o_ref, l_ref) in enumerate(zip(o_refs, l_refs)):
        d = PATTERNS[g][1]
        lses.append(_load_interleaved(l_ref, scr_l.at[g], d, tm)[0])
        outs.append(_load_interleaved(o_ref, scr_o.at[g], d, tm))
    m = jnp.maximum(jnp.maximum(lses[0], lses[1]), lses[2])
    es = [jnp.exp(l - m) for l in lses]
    inv = 1.0 / (es[0] + es[1] + es[2])
    heads = []
    for h in range(N_HEADS):
        acc = None
        for g in range(len(PATTERNS)):
            w = (es[g] * inv)[:, h:h + 1]
            term = w * outs[g][h]
            acc = term if acc is None else acc + term
        heads.append(acc)
    return jnp.concatenate(heads, axis=1)


def _dsw_decode_kernel(q_ref, kc_ref, vc_ref, kn_ref, vn_ref, o_ref, *, n_new, cache_len):
    scale = HEAD_DIM ** -0.5
    for i in range(n_new):
        outs, lses = [], []
        for g, (_, d) in enumerate(PATTERNS):
            q = q_ref[i, g] * scale
            n_self = (i // d) + 1
            new_rows = [i - d * j for j in range(n_self)]
            n_old = N_KEYS + 1 - n_self
            start = cache_len + i - d * N_KEYS
            if d == 1:
                k_old = kc_ref[start:start + n_old]
                v_old = vc_ref[start:start + n_old]
            else:
                k_old = kc_ref[pl.ds(start, n_old, stride=d)]
                v_old = vc_ref[pl.ds(start, n_old, stride=d)]
            k_new = jnp.stack([kn_ref[r] for r in new_rows])
            v_new = jnp.stack([vn_ref[r] for r in new_rows])
            s_old = jnp.sum(k_old * q[None], axis=-1, keepdims=True)
            s_new = jnp.sum(k_new * q[None], axis=-1, keepdims=True)
            m = jnp.maximum(jnp.max(s_old, axis=0), jnp.max(s_new, axis=0))
            p_old = jnp.exp(s_old - m[None])
            p_new = jnp.exp(s_new - m[None])
            den = jnp.sum(p_old, axis=0) + jnp.sum(p_new, axis=0)
            acc = jnp.sum(p_old * v_old, axis=0) + jnp.sum(p_new * v_new, axis=0)
            outs.append(acc / den)
            lses.append(m + jnp.log(den))
        mm = jnp.maximum(jnp.maximum(lses[0], lses[1]), lses[2])
        es = [jnp.exp(l - mm) for l in lses]
        inv = 1.0 / (es[0] + es[1] + es[2])
        o_ref[i] = (es[0] * inv) * outs[0] + (es[1] * inv) * outs[1] + (es[2] * inv) * outs[2]


def _dsw_decode_shift_kernel(q_ref, kc_ref, vc_ref, kn_ref, vn_ref, o_ref, nk_hbm, nv_hbm, sem, *, n_new, cache_len):
    b = pl.program_id(0)
    keep = cache_len - n_new
    copies = [
        pltpu.make_async_copy(kc_ref.at[:, pl.ds(n_new, keep)], nk_hbm.at[pl.ds(b, 1), pl.ds(0, keep)], sem.at[0]),
        pltpu.make_async_copy(kn_ref, nk_hbm.at[pl.ds(b, 1), pl.ds(keep, n_new)], sem.at[1]),
        pltpu.make_async_copy(vc_ref.at[:, pl.ds(n_new, keep)], nv_hbm.at[pl.ds(b, 1), pl.ds(0, keep)], sem.at[2]),
        pltpu.make_async_copy(vn_ref, nv_hbm.at[pl.ds(b, 1), pl.ds(keep, n_new)], sem.at[3]),
    ]
    for c in copies:
        c.start()
    _dsw_decode_kernel(q_ref, kc_ref.at[0], vc_ref.at[0], kn_ref.at[0], vn_ref.at[0], o_ref,
                       n_new=n_new, cache_len=cache_len)
    for c in copies:
        c.wait()


def dsw_decode(q, cache_k, cache_v, k_new, v_new, emit_caches=False):
    n_batch, n_new = q.shape[:2]
    cache_len = cache_k.shape[1]
    assert n_new <= min(d for _, d in PATTERNS[1:]) and cache_len >= max(w for w, _ in PATTERNS)
    b5 = lambda b: (b, 0, 0, 0, 0)
    b4 = lambda b: (b, 0, 0, 0)
    in_specs = [pl.BlockSpec((None,) + q.shape[1:], b5),
                pl.BlockSpec((None,) + cache_k.shape[1:], b4),
                pl.BlockSpec((None,) + cache_v.shape[1:], b4),
                pl.BlockSpec((None,) + k_new.shape[1:], b4),
                pl.BlockSpec((None,) + v_new.shape[1:], b4)]
    o_spec = pl.BlockSpec((None,) + k_new.shape[1:], b4)
    o_shape = jax.ShapeDtypeStruct(k_new.shape, F32)
    if not emit_caches:
        return pl.pallas_call(
            functools.partial(_dsw_decode_kernel, n_new=n_new, cache_len=cache_len),
            grid=(n_batch,), in_specs=in_specs, out_specs=o_spec, out_shape=o_shape,
            compiler_params=_cparams(1), name="dsw_decode",
        )(q, cache_k, cache_v, k_new, v_new)
    any_spec = pl.BlockSpec(memory_space=pl.ANY)
    cache_shape = jax.ShapeDtypeStruct(cache_k.shape, F32)
    in_specs[1:] = [pl.BlockSpec((1,) + a.shape[1:], b4) for a in (cache_k, cache_v, k_new, v_new)]
    return pl.pallas_call(
        functools.partial(_dsw_decode_shift_kernel, n_new=n_new, cache_len=cache_len),
        grid=(n_batch,), in_specs=in_specs, out_specs=[o_spec, any_spec, any_spec],
        out_shape=[o_shape, cache_shape, cache_shape],
        scratch_shapes=[pltpu.SemaphoreType.DMA((4,))],
        compiler_params=_cparams(1), name="dsw_decode_shift",
    )(q, cache_k, cache_v, k_new, v_new)


def _route(logit):
    lane = lax.broadcasted_iota(jnp.int32, logit.shape, 1)
    lanef = lane.astype(F32)
    big = 1e9
    is_g = jnp.logical_and(lane >= N_EXPERTS, lane < N_EXPERTS + N_GROUPS)
    lg = jnp.where(is_g, logit, NEG)
    mg = jnp.max(lg, axis=1, keepdims=True)
    gsel = jnp.min(jnp.where(lg == mg, lanef, big), axis=1, keepdims=True) - float(N_EXPERTS)
    p_group = 1.0 / jnp.sum(jnp.exp(lg - mg), axis=1, keepdims=True)
    assert EXPERTS_PER_GROUP == 4
    egrp = jnp.right_shift(lane, 2).astype(F32)
    in_grp = jnp.logical_and(lane < N_EXPERTS, egrp == gsel)
    le = jnp.where(in_grp, logit, NEG)
    v1 = jnp.max(le, axis=1, keepdims=True)
    e1 = jnp.min(jnp.where(le == v1, lanef, big), axis=1, keepdims=True)
    le2 = jnp.where(lanef == e1, NEG, le)
    v2 = jnp.max(le2, axis=1, keepdims=True)
    e2 = jnp.min(jnp.where(jnp.logical_and(le2 == v2, lanef != e1), lanef, big), axis=1, keepdims=True)
    t = jnp.exp(v2 - v1)
    w1 = 1.0 / (1.0 + t)
    w2 = t * w1
    gate = jnp.where(lanef == e1, p_group * w1, 0.0) + jnp.where(lanef == e2, p_group * w2, 0.0)
    return jnp.where(lane == GROUP_LANE, gsel, gate)


def _proj_ln_route_kernel(y_ref, *rest, alpha):
    _proj_ln_route_body(y_ref[...], *rest, alpha=alpha)


def _merge_proj_ln_route_kernel(o0_ref, o1_ref, o2_ref, l0_ref, l1_ref, l2_ref, wo_ref, x_ref, g_ref, b_ref,
                                wr_ref, br_ref, xg_ref, scr_o, scr_l, *, alpha):
    y = _merge_patterns((o0_ref, o1_ref, o2_ref), (l0_ref, l1_ref, l2_ref), scr_o, scr_l, x_ref.shape[0])
    _proj_ln_route_body(y.astype(BF16), wo_ref, x_ref, g_ref, b_ref, wr_ref, br_ref, xg_ref, alpha=alpha)


def _proj_ln_route_body(y, wo_ref, x_ref, g_ref, b_ref, wr_ref, br_ref, xg_ref, *, alpha):
    d_model = x_ref.shape[1]
    h = jnp.dot(y, wo_ref[...], preferred_element_type=F32)
    x1 = _layer_norm(alpha * x_ref[...] + h, g_ref[...], b_ref[...])
    xg_ref[:, :d_model] = x1
    hi = x1.astype(BF16)
    lo = (x1 - hi.astype(F32)).astype(BF16)
    part = jnp.dot(jnp.concatenate([hi, lo], axis=1), wr_ref[...], preferred_element_type=F32)
    logit = part[:, :LANES] + part[:, LANES:] + br_ref[...]
    xg_ref[:, d_model:] = _route(logit)


def proj_ln_route(y, w_o, x, g, b, w_r2, b_r, alpha):
    T, D = x.shape
    tm = _tile(T, 512)
    row = lambda i: (i, 0)
    const = lambda i: (0, 0)
    return pl.pallas_call(
        functools.partial(_proj_ln_route_kernel, alpha=alpha),
        grid=(T // tm,),
        in_specs=[pl.BlockSpec((tm, y.shape[1]), row), pl.BlockSpec(w_o.shape, const),
                  pl.BlockSpec((tm, D), row), pl.BlockSpec((1, D), const), pl.BlockSpec((1, D), const),
                  pl.BlockSpec(w_r2.shape, const), pl.BlockSpec((1, LANES), const)],
        out_specs=pl.BlockSpec((tm, D + LANES), row),
        out_shape=jax.ShapeDtypeStruct((T, D + LANES), F32),
        compiler_params=_cparams(1),
        name="proj_ln_route",
    )(y, w_o, x, g, b, w_r2, b_r)


def merge_proj_ln_route(outs, lses, n_batch, seq, w_o, x, g, b, w_r2, b_r, alpha):
    T, D = x.shape
    width = N_HEADS * HEAD_DIM
    tm = _tile(seq, 512)
    nt = seq // tm
    dils = [d for _, d in PATTERNS]
    row = lambda bi, i: (bi * nt + i, 0)
    const = lambda bi, i: (0, 0)
    return pl.pallas_call(
        functools.partial(_merge_proj_ln_route_kernel, alpha=alpha),
        grid=(n_batch, nt),
        in_specs=[_dec_spec(d, tm, width) for d in dils] + [_dec_spec(d, tm, LANES) for d in dils]
        + [pl.BlockSpec(w_o.shape, const), pl.BlockSpec((tm, D), row), pl.BlockSpec((1, D), const),
           pl.BlockSpec((1, D), const), pl.BlockSpec(w_r2.shape, const), pl.BlockSpec((1, LANES), const)],
        out_specs=pl.BlockSpec((tm, D + LANES), row),
        out_shape=jax.ShapeDtypeStruct((T, D + LANES), F32),
        scratch_shapes=[pltpu.VMEM((len(dils), width // LANES, tm, LANES), F32),
                        pltpu.VMEM((len(dils), 1, tm, LANES), F32)],
        compiler_params=_cparams(2),
        name="merge_proj_ln_route",
    )(*outs, *lses, w_o, x, g, b, w_r2, b_r)


def _expert_ffn(xb, gate, e, wu, wd):
    h = jnp.dot(xb, wu, preferred_element_type=F32)
    hg, hu = h[:, :D_EXPERT], h[:, D_EXPERT:]
    lane = lax.broadcasted_iota(jnp.int32, gate.shape, 1)
    ge = jnp.sum(jnp.where(lane == e, gate, 0.0), axis=1, keepdims=True)
    a = (hg * jax.nn.sigmoid(hg) * hu * ge).astype(BF16)
    return jnp.dot(a, wd, preferred_element_type=F32)


def _moe_dense_kernel(x_ref, gate_ref, wu_ref, wd_ref, f_ref):
    e = pl.program_id(1)

    @pl.when(e == 0)
    def _():
        f_ref[...] = jnp.zeros_like(f_ref)

    f_ref[...] += _expert_ffn(x_ref[...].astype(BF16), gate_ref[...], e, wu_ref[...], wd_ref[...])


def moe_dense(xg, w_up, w_down, layer):
    T, D = xg.shape[0], w_up.shape[2]
    tm = _tile(T, 1024)
    return pl.pallas_call(
        _moe_dense_kernel,
        grid=(T // tm, N_EXPERTS),
        in_specs=[pl.BlockSpec((tm, D), lambda i, e: (i, 0)),
                  pl.BlockSpec((tm, LANES), lambda i, e: (i, D // LANES)),
                  pl.BlockSpec((None, None, D, 2 * D_EXPERT), lambda i, e: (layer, e, 0, 0)),
                  pl.BlockSpec((None, None, D_EXPERT, D), lambda i, e: (layer, e, 0, 0))],
        out_specs=pl.BlockSpec((tm, D), lambda i, e: (i, 0)),
        out_shape=jax.ShapeDtypeStruct((T, D), F32),
        compiler_params=_cparams(2),
        name="moe_dense",
    )(xg, xg, w_up, w_down)


def _moe_plan(gsel, tm):
    T = gsel.shape[0]
    nt = T // tm + N_GROUPS
    groups = jnp.arange(N_GROUPS, dtype=jnp.int32)
    order = jnp.argsort(gsel, stable=True).astype(jnp.int32)
    counts = jnp.sum((gsel[:, None] == groups[None, :]).astype(jnp.int32), axis=0)
    cstart = jnp.cumsum(counts) - counts
    ntile = (counts + tm - 1) // tm
    tend = jnp.cumsum(ntile)
    tile = jnp.arange(nt, dtype=jnp.int32)
    tg = jnp.sum((tile[:, None] >= tend[None, :]).astype(jnp.int32), axis=1)
    used = tg < N_GROUPS
    tg = jnp.minimum(tg, N_GROUPS - 1)
    base = (tile - (tend - ntile)[tg]) * tm
    rows = jnp.where(used, jnp.clip(counts[tg] - base, 0, tm), 0).astype(jnp.int32)
    r = jnp.arange(tm, dtype=jnp.int32)
    pos = jnp.clip(cstart[tg][:, None] + base[:, None] + r[None, :], 0, T - 1)
    return tg.astype(jnp.int32), rows, order[pos].reshape(nt * tm)


def _moe_sorted_kernel(tg_ref, rows_ref, idx_ref, xg_hbm, wu_ref, wd_ref, y_hbm,
                       xbuf0, xbuf1, acc0, acc1, xb16, gsem, ssem, *, tm, n_tiles):
    j = pl.program_id(0)
    e = pl.program_id(1)
    n_e = EXPERTS_PER_GROUP
    part = tm // n_e
    d_model = xb16.shape[1]
    rows_j = rows_ref[j]
    rows_prev = rows_ref[jnp.maximum(j - 1, 0)]
    rows_next = rows_ref[jnp.minimum(j + 1, n_tiles - 1)]
    active = rows_j > 0
    has_next = jnp.logical_and(j + 1 < n_tiles, rows_next > 0)
    has_prev = jnp.logical_and(j >= 1, rows_prev > 0)
    fast = jnp.logical_and(jnp.logical_and(active, has_next), jnp.logical_and(j >= 1, rows_prev == tm))
    slow = jnp.logical_not(fast)
    last_step = jnp.logical_and(j == n_tiles - 1, e == n_e - 1)
    bufs = ((xbuf0, acc0), (xbuf1, acc1))

    def gather_row(tile, r, p):
        tok = idx_ref[tile * tm + r]
        pltpu.make_async_copy(xg_hbm.at[pl.ds(tok, 1)], bufs[p][0].at[pl.ds(r, 1)], gsem.at[p]).start()

    def scatter_row(tile, r, p):
        tok = idx_ref[tile * tm + r]
        pltpu.make_async_copy(bufs[p][1].at[pl.ds(r, 1)], y_hbm.at[pl.ds(tok, 1)], ssem.at[p]).start()

    def gather_all(tile, p):
        def body(r, carry):
            gather_row(tile, r, p)
            return carry
        lax.fori_loop(0, tm, body, 0, unroll=8)

    def gather_wait(p):
        pltpu.make_async_copy(xg_hbm.at[pl.ds(0, tm)], bufs[p][0], gsem.at[p]).wait()

    def scatter_all(tile, p):
        n = rows_ref[tile]

        def body(r, carry):
            scatter_row(tile, r, p)
            return carry

        @pl.when(n == tm)
        def _():
            lax.fori_loop(0, tm, body, 0, unroll=8)

        @pl.when(n < tm)
        def _():
            lax.fori_loop(0, n, body, 0)

    def scatter_wait(tile, p):
        n = rows_ref[tile]

        @pl.when(n == tm)
        def _():
            pltpu.make_async_copy(bufs[p][1], y_hbm.at[pl.ds(0, tm)], ssem.at[p]).wait()

        @pl.when(n < tm)
        def _():
            def body(r, carry):
                pltpu.make_async_copy(bufs[p][1].at[pl.ds(0, 1)], y_hbm.at[pl.ds(0, 1)], ssem.at[p]).wait()
                return carry
            lax.fori_loop(0, n, body, 0)

    def expert_step(p):
        xbuf, acc = bufs[p]
        gate = xbuf[:, d_model:]
        acc[...] += _expert_ffn(xb16[...], gate, tg_ref[j] * n_e + e, wu_ref[...], wd_ref[...])

    def tile_steps(p):
        q = 1 - p

        @pl.when(e == 0)
        def _():
            @pl.when(jnp.logical_and(j == 0, active))
            def _():
                gather_all(0, p)

            @pl.when(j >= 2)
            def _():
                scatter_wait(jnp.maximum(j - 2, 0), p)

            @pl.when(active)
            def _():
                gather_wait(p)
                xb16[...] = bufs[p][0][:, :d_model].astype(BF16)
                bufs[p][1][...] = jnp.zeros(bufs[p][1].shape, F32)

            @pl.when(jnp.logical_and(slow, has_next))
            def _():
                gather_all(j + 1, q)

            @pl.when(jnp.logical_and(slow, has_prev))
            def _():
                scatter_all(jnp.maximum(j - 1, 0), q)

        @pl.when(fast)
        def _():
            expert_step(p)
            for r in range(part):
                gather_row(j + 1, e * part + r, q)
                scatter_row(j - 1, e * part + r, q)

        @pl.when(jnp.logical_and(slow, active))
        def _():
            expert_step(p)

        @pl.when(last_step)
        def _():
            scatter_all(n_tiles - 1, p)
            scatter_wait(n_tiles - 2, q)
            scatter_wait(n_tiles - 1, p)

    for parity in (0, 1):
        pl.when(j % 2 == parity)(functools.partial(tile_steps, parity))


def moe_sorted(xg, w_up, w_down, layer):
    T, D = xg.shape[0], w_up.shape[2]
    tm = MOE_TM
    gsel = xg[:, D + GROUP_LANE].astype(jnp.int32)
    tile_group, tile_rows, idx = _moe_plan(gsel, tm)
    nt = tile_group.shape[0]
    w_idx = lambda j, e, tg, rows, idx: (layer, tg[j] * EXPERTS_PER_GROUP + e, 0, 0)
    grid_spec = pltpu.PrefetchScalarGridSpec(
        num_scalar_prefetch=3,
        grid=(nt, EXPERTS_PER_GROUP),
        in_specs=[pl.BlockSpec(memory_space=pl.ANY),
                  pl.BlockSpec((None, None, D, 2 * D_EXPERT), w_idx),
                  pl.BlockSpec((None, None, D_EXPERT, D), w_idx)],
        out_specs=pl.BlockSpec(memory_space=pl.ANY),
        scratch_shapes=[pltpu.VMEM((tm, D + LANES), F32), pltpu.VMEM((tm, D + LANES), F32),
                        pltpu.VMEM((tm, D), F32), pltpu.VMEM((tm, D), F32), pltpu.VMEM((tm, D), BF16),
                        pltpu.SemaphoreType.DMA((2,)), pltpu.SemaphoreType.DMA((2,))],
    )
    return pl.pallas_call(
        functools.partial(_moe_sorted_kernel, tm=tm, n_tiles=nt),
        grid_spec=grid_spec,
        out_shape=jax.ShapeDtypeStruct((T, D), F32),
        compiler_params=_cparams(2),
        name="moe_sorted",
    )(tile_group, tile_rows, idx, xg, w_up, w_down)


def _post_kernel(x1_ref, f_ref, p_ref, wpg_ref, wpp_ref, g_ref, b_ref, x2_ref, x2b_ref, *, alpha):
    x1 = x1_ref[...]
    gatev = jax.nn.sigmoid(jnp.dot(x1.astype(BF16), wpg_ref[...], preferred_element_type=F32))
    pp = jnp.dot(p_ref[...].astype(BF16), wpp_ref[...], preferred_element_type=F32)
    x2 = _layer_norm(alpha * x1 + f_ref[...] + gatev * pp, g_ref[...], b_ref[...])
    x2_ref[...] = x2
    x2b_ref[...] = x2.astype(BF16)


def post(xg, f, p, layer, w_pg, w_pp, g, b, alpha):
    T, D = xg.shape[0], w_pg.shape[0]
    tm = _tile(T, 512)
    row = lambda i: (i, 0)
    const = lambda i: (0, 0)
    return pl.pallas_call(
        functools.partial(_post_kernel, alpha=alpha),
        grid=(T // tm,),
        in_specs=[pl.BlockSpec((tm, D), row), pl.BlockSpec((tm, D), row),
                  pl.BlockSpec((None, tm, p.shape[-1]), lambda i: (layer, i, 0)),
                  pl.BlockSpec(w_pg.shape, const), pl.BlockSpec(w_pp.shape, const),
                  pl.BlockSpec((1, D), const), pl.BlockSpec((1, D), const)],
        out_specs=[pl.BlockSpec((tm, D), row), pl.BlockSpec((tm, D), row)],
        out_shape=[jax.ShapeDtypeStruct((T, D), F32), jax.ShapeDtypeStruct((T, D), BF16)],
        compiler_params=_cparams(1),
        name="post",
    )(xg, f, p, w_pg, w_pp, g, b)


def kernel(x_prompt, x_sample, state_ret, cache_k, cache_v, p_prompt, p_sample, w_in_a, w_o_a, w_kv, w_q_b, w_o_b,
           ln_g, ln_b, w_route_group, b_route_group, w_route_expert, b_route_expert, w_expert_up, w_expert_down,
           w_ple_gate, w_ple_proj):
    B, L, D = x_prompt.shape
    Bs, Ls, _ = x_sample.shape
    n_a, n_b = w_in_a.shape[0], w_q_b.shape[0]
    depth = n_a + n_b
    alpha = (2 * depth) ** 0.25
    width = N_HEADS * HEAD_DIM
    cache_len = cache_k.shape[1]
    assert n_a >= 1 and n_b >= 1
    assert D == width and L %(max(RET_CHUNK, DSW_TQ) * PATTERNS[-1][1]) == 0 and Ls <= RET_CHUNK

    w_in_b, w_o_a_b, w_kv_b = w_in_a.astype(BF16), w_o_a.astype(BF16), w_kv.astype(BF16)
    w_q_bb, w_o_b_b = w_q_b.astype(BF16), w_o_b.astype(BF16)
    w_up_b, w_down_b = w_expert_up.astype(BF16), w_expert_down.astype(BF16)
    w_pg_b, w_pp_b = w_ple_gate.astype(BF16), w_ple_proj.astype(BF16)
    pad = LANES - N_EXPERTS - N_GROUPS
    w_r = jnp.concatenate([w_route_expert, w_route_group, jnp.zeros((depth, D, pad), F32)], axis=-1)
    w_r_hi = w_r.astype(BF16)
    w_r_lo = (w_r - w_r_hi.astype(F32)).astype(BF16)
    w_r2 = jnp.concatenate([w_r_hi, w_r_lo], axis=-1)
    w_r2 = jnp.concatenate([w_r2, w_r2], axis=1)
    b_r =jnp.concatenate([b_route_expert, b_route_group, jnp.zeros((depth, pad), F32)], axis=-1)[:, None, :]

    pp = p_prompt.reshape(depth, B * L, -1)
    ps = p_sample.reshape(depth, Bs * Ls, -1)
    streams = {
        "p": [x_prompt.reshape(B * L, D), None, pp],
        "s": [x_sample.reshape(Bs * Ls, D), None, ps],
    }
    for st in streams.values():
        st[1] = st[0].astype(BF16)

    ret_p, ret_s = [], []
    kv_state = None
    for i in range(depth):
        mixed = {}
        if i < n_a:
            qkvg = matmul(streams["p"][1], w_in_b[i], BF16)
            y, s_fin = retention(qkvg, B, L // RET_CHUNK, RET_CHUNK, 0)
            mixed["p"] = y
            ret_p.append(s_fin)
            qkvg = matmul(streams["s"][1], w_in_b[i], BF16)
            qkvg = jnp.pad(qkvg.reshape(Bs, Ls, -1), ((0, 0), (0, RET_CHUNK - Ls), (0, 0)))
            y, s_fin = retention(qkvg.reshape(Bs * RET_CHUNK, -1), Bs, 1, Ls, PAST_LEN, state=state_ret, layer=i)
            mixed["s"] = y.reshape(Bs, RET_CHUNK, -1)[:, :Ls].reshape(Bs * Ls, -1)
            ret_s.append(s_fin)
            w_o = w_o_a_b[i]
        else:
            j = i - n_a
            (k_decs, v_decs), (k_s, v_s) = kv_state
            qs = q_projection(streams["p"][1], w_q_bb[j], B, L)
            mixed["p"] = tuple(zip(*[dsw_pattern(qs[g], k_decs[g], v_decs[g]) for g in range(len(PATTERNS))]))
            q_s = matmul(streams["s"][1], w_q_bb[j], F32).reshape(Bs, Ls, len(PATTERNS), N_HEADS, HEAD_DIM)
            if j == 0:
                o_s, new_k_s, new_v_s = dsw_decode(q_s, cache_k, cache_v, k_s, v_s, emit_caches=True)
            else:
                o_s = dsw_decode(q_s, cache_k, cache_v, k_s, v_s)
            mixed["s"] = o_s.reshape(Bs * Ls, width).astype(BF16)
            w_o = w_o_b_b[j]
        for name, st in streams.items():
            x, _, p = st
            route_args = (w_o, x, ln_g[i, 0:1], ln_b[i, 0:1], w_r2[i], b_r[i], alpha)
            if isinstance(mixed[name], tuple):
                xg = merge_proj_ln_route(*mixed[name], B, L, *route_args)
            else:
                xg = proj_ln_route(mixed[name], *route_args)
            sortable = xg.shape[0] % MOE_TM == 0 and xg.shape[0] >= 8 * MOE_TM
            f = (moe_sorted if sortable else moe_dense)(xg, w_up_b, w_down_b, i)
            st[0], st[1] = post(xg, f, p, i, w_pg_b[i], w_pp_b[i], ln_g[i, 1:2], ln_b[i, 1:2], alpha)
        if i == n_a - 1:
            kv_p, k_decs, v_decs = kv_projection(streams["p"][1], w_kv_b, B, L)
            kv_s = matmul(streams["s"][1], w_kv_b, F32)
            k_s = kv_s[:, :width].reshape(Bs, Ls, N_HEADS, HEAD_DIM)
            v_s = kv_s[:, width:].reshape(Bs, Ls, N_HEADS, HEAD_DIM)
            kv_state = ((k_decs, v_decs), (k_s, v_s))
            wb_p = min(W_MAX, L)
            kv_tail = kv_p.reshape(B, L, 2 * width)[:, L - wb_p:]
            new_k_p = kv_tail[..., :width].reshape(B, wb_p, N_HEADS, HEAD_DIM)
            new_v_p = kv_tail[..., width:].reshape(B, wb_p, N_HEADS, HEAD_DIM)

    return (streams["p"][0].reshape(B, L, D), streams["s"][0].reshape(Bs, Ls, D),
            jnp.stack(ret_p), jnp.stack(ret_s), new_k_p, new_v_p, new_k_s, new_v_s)
```
